```python
import jax, jax.numpy as jnp
from jax import lax
import numpy as np

D_MODEL = 4096
BATCH = 4
SEQ = 2048
DEPTH = 2
DEC_BATCH = 128
DEC_SEQ = 1
PAST_LEN = 16384
PAGE_SIZE = 128

M_HEAD_DIM = 64
M_HEADS = D_MODEL // 128
M_INNER = M_HEADS * M_HEAD_DIM
M_GROUPS = 4
M_STATE = 128
CONV_WIDTH = 4
CONV_DIM = M_INNER + 2 * M_GROUPS * M_STATE
CHUNK = 128
A_HEADS = D_MODEL // 256
QK_NOPE_DIM = 128
QK_ROPE_DIM = 64
V_HEAD_DIM = 128
Q_LORA_RANK = 3 * D_MODEL // 16
KV_LORA_RANK = D_MODEL // 16
A_WIDTH = A_HEADS * V_HEAD_DIM
Q_BLOCK = 128
R_HEADS = D_MODEL // 512
R_DK = 128
R_DV = 256
R_QK_WIDTH = R_HEADS * R_DK
R_V_WIDTH = R_HEADS * R_DV
D_FF = -(-8 * D_MODEL // (3 * 256)) * 256
ROPE_BASE = 10000.0
EPS = 1e-6
GN_EPS = 1e-5
N_BRANCH = 3
IN_WIDTHS = (M_INNER, CONV_DIM, M_HEADS,
             Q_LORA_RANK, KV_LORA_RANK, QK_ROPE_DIM,
             R_QK_WIDTH, R_QK_WIDTH, R_V_WIDTH, R_V_WIDTH)
D_IN = sum(IN_WIDTHS)

kernel_name = 'hybrid_ssd_mla_retention_decoder_step'

F32 = jnp.float32


def _split(x, widths):
    idx = np.cumsum(widths)[:-1].tolist()
    return jnp.split(x, idx, axis=-1)


def _chunk_len(n, block=CHUNK):
    return block if n % block == 0 else n


def _rmsnorm(x, g):
    x32 = x.astype(F32)
    y = x32 * lax.rsqrt(jnp.mean(x32 * x32, axis=-1, keepdims=True) + EPS)
    return (y * g.astype(F32)).astype(x.dtype)


def _gated_group_rmsnorm(y, z, g, n_groups):
    v = y.astype(F32) * jax.nn.silu(z.astype(F32))
    shp = v.shape
    v = v.reshape(shp[:-1] + (n_groups, shp[-1] // n_groups))
    v = v * lax.rsqrt(jnp.mean(v * v, axis=-1, keepdims=True) + EPS)
    return v.reshape(shp) * g.astype(F32)


def _head_groupnorm(o, g):
    mu = jnp.mean(o, axis=-1, keepdims=True)
    var = jnp.mean(jnp.square(o - mu), axis=-1, keepdims=True)
    o = (o - mu) * lax.rsqrt(var + GN_EPS)
    return o.reshape(o.shape[:2] + (-1,)) * g.astype(F32)


def _rope(x, pos):
    d = x.shape[-1]
    half = d // 2
    inv = ROPE_BASE ** (-2.0 * jnp.arange(half, dtype=F32) / d)
    ang = pos.astype(F32)[:, None] * inv[None, :]
    cos = jnp.cos(ang)[None, :, None, :]
    sin = jnp.sin(ang)[None, :, None, :]
    x32 = x.astype(F32)
    x1, x2 = x32[..., :half], x32[..., half:]
    return jnp.concatenate([x1 * cos - x2 * sin, x2 * cos + x1 * sin], axis=-1).astype(x.dtype)


def _causal_dwconv(full, w, b):
    out = lax.conv_general_dilated(full, w[:, None, :].astype(full.dtype), window_strides=(1,),
                                   padding='VALID', dimension_numbers=('NWC', 'WIO', 'NWC'),
                                   feature_group_count=full.shape[-1])
    return out + b.astype(full.dtype)


def _ssd(x, dt, a, b_in, c_in, h0):
    bsz, L, H, P = x.shape
    G, N = b_in.shape[-2:]
    E = H // G
    Q = _chunk_len(L)
    C = L // Q
    xd = (x.astype(F32) * dt[..., None]).reshape(bsz, C, Q, G, E, P)
    la = (dt * a).reshape(bsz, C, Q, G, E)
    bc = b_in.astype(F32).reshape(bsz, C, Q, G, N)
    cc = c_in.astype(F32).reshape(bsz, C, Q, G, N)
    cum = jnp.cumsum(la, axis=2)
    causal = jnp.tril(jnp.ones((Q, Q), bool))[None, None, :, :, None, None]
    seg = jnp.exp(jnp.where(causal, cum[:, :, :, None] - cum[:, :, None, :], -jnp.inf))
    cb = jnp.einsum('bclgn,bcsgn->bclsg', cc, bc)
    y_diag = jnp.einsum('bclsge,bcsgep->bclgep', cb[..., None] * seg, xd)
    xw = xd * jnp.exp(cum[:, :, -1:] - cum)[..., None]
    chunk_states = jnp.einsum('bclgn,bclgep->bcgepn', bc, xw)
    chunk_decay = jnp.exp(cum[:, :, -1])

    def step(h, inp):
        st, dec = inp
        return h * dec[..., None, None] + st, h

    h_last, h_prev = lax.scan(step, h0.astype(F32).reshape(bsz, G, E, P, N),
                              (jnp.moveaxis(chunk_states, 1, 0), jnp.moveaxis(chunk_decay, 1, 0)))
    h_prev = jnp.moveaxis(h_prev, 0, 1)
    y_off = jnp.einsum('bclgn,bcgepn->bclgep', cc, h_prev) * jnp.exp(cum)[..., None]
    y = (y_diag + y_off).reshape(bsz, L, H, P)
    return y, h_last.reshape(bsz, H, P, N)


def _retention(q, k, v, s0):
    bsz, L, H, DK = q.shape
    DV = v.shape[-1]
    Q = _chunk_len(L)
    C = L // Q
    log_g = jnp.log1p(-jnp.exp2(-5.0 - jnp.arange(H, dtype=F32)))
    qc = q.astype(F32).reshape(bsz, C, Q, H, DK)
    kc = k.astype(F32).reshape(bsz, C, Q, H, DK)
    vc = v.astype(F32).reshape(bsz, C, Q, H, DV)
    t = jnp.arange(Q, dtype=F32)
    rel = t[:, None] - t[None, :]
    dmat = jnp.where((rel >= 0)[..., None], jnp.exp(jnp.maximum(rel, 0.0)[..., None] * log_g), 0.0)
    scores = jnp.einsum('bclhd,bcshd->bclsh', qc, kc) * dmat
    o_intra = jnp.einsum('bclsh,bcshe->bclhe', scores, vc)
    k_w = kc * jnp.exp((Q - 1 - t)[:, None] * log_g)[..., None]
    chunk_states = jnp.einsum('bcshd,bcshe->bchde', k_w, vc)
    chunk_decay = jnp.exp(Q * log_g)

    def step(s, st):
        return s * chunk_decay[:, None, None] + st, s

    s_last, s_prev = lax.scan(step, s0.astype(F32), jnp.moveaxis(chunk_states, 1, 0))
    s_prev = jnp.moveaxis(s_prev, 0, 1)
    o_cross = jnp.einsum('bclhd,bchde->bclhe', qc, s_prev) * jnp.exp((t + 1.0)[:, None] * log_g)[..., None]
    return (o_intra + o_cross).reshape(bsz, L, H, DV), s_last


def _mla_prompt(q_nope, q_pe, ckv, kpe, w_ukv):
    bsz, S, H, _ = q_nope.shape
    kv = jnp.einsum('bkr,rhd->bkhd', ckv, w_ukv.reshape(KV_LORA_RANK, H, QK_NOPE_DIM + V_HEAD_DIM))
    k_nope, v = kv[..., :QK_NOPE_DIM], kv[..., QK_NOPE_DIM:]
    qb = _chunk_len(S, Q_BLOCK)
    nb = S // qb
    qn = jnp.moveaxis(q_nope.reshape(bsz, nb, qb, H, QK_NOPE_DIM), 1, 0)
    qp = jnp.moveaxis(q_pe.reshape(bsz, nb, qb, H, QK_ROPE_DIM), 1, 0)
    kpos = jnp.arange(S)
    scale = (QK_NOPE_DIM + QK_ROPE_DIM) ** -0.5

    def block(args):
        qn_b, qp_b, i = args
        s = jnp.einsum('bqhd,bkhd->bhqk', qn_b, k_nope) + jnp.einsum('bqhd,bkd->bhqk', qp_b, kpe)
        qpos = i * qb + jnp.arange(qb)
        s = jnp.where(kpos[None, :] <= qpos[:, None], s.astype(F32) * scale, -jnp.inf)
        p = jax.nn.softmax(s, axis=-1).astype(v.dtype)
        return jnp.einsum('bhqk,bkhd->bqhd', p, v)

    o = lax.map(block, (qn, qp, jnp.arange(nb)))
    return jnp.moveaxis(o, 0, 1).reshape(bsz, S, H * V_HEAD_DIM)


def _mla_sample(q_nope, q_pe, ckv, kpe, w_ukv, cache_ckv, cache_kpe, page_table, layer):
    bsz, Lq, H, _ = q_nope.shape
    ckv_past = cache_ckv[layer, page_table].reshape(bsz, -1, KV_LORA_RANK)
    kpe_past = cache_kpe[layer, page_table].reshape(bsz, -1, QK_ROPE_DIM)
    n_past = ckv_past.shape[1]
    scale = (QK_NOPE_DIM + QK_ROPE_DIM) ** -0.5
    w = w_ukv.reshape(KV_LORA_RANK, H, QK_NOPE_DIM + V_HEAD_DIM)
    w_uk, w_uv = w[..., :QK_NOPE_DIM], w[..., QK_NOPE_DIM:]
    q_lat = jnp.einsum('bqhd,rhd->bqhr', q_nope, w_uk)
    s_past = jnp.einsum('bqhr,bkr->bhqk', q_lat, ckv_past) + jnp.einsum('bqhd,bkd->bhqk', q_pe, kpe_past)
    s_new = jnp.einsum('bqhr,bkr->bhqk', q_lat, ckv) + jnp.einsum('bqhd,bkd->bhqk', q_pe, kpe)
    causal = jnp.tril(jnp.ones((Lq, Lq), bool))
    s_new = jnp.where(causal, s_new.astype(F32) * scale, -jnp.inf)
    s = jnp.concatenate([s_past.astype(F32) * scale, s_new], axis=-1)
    p = jax.nn.softmax(s, axis=-1).astype(ckv.dtype)
    o_lat = (jnp.einsum('bhqk,bkr->bqhr', p[..., :n_past], ckv_past)
             + jnp.einsum('bhqk,bkr->bqhr', p[..., n_past:], ckv))
    o = jnp.einsum('bqhr,rhd->bqhd', o_lat, w_uv)
    return o.reshape(bsz, Lq, H * V_HEAD_DIM)


def _mixer_block(x, pos, conv_buf, ssm_h, ret_s, paged,
                 norm_g, w_in, conv_w, conv_b, dt_bias, a_log, d_skip, m_norm_g,
                 q_norm_g, w_uq, kv_norm_g, w_ukv, r_norm_g,
                 w_br_m, w_br_a, w_br_r, w_gate, b_gate, w_out):
    bsz, L, _ = x.shape
    dtype = x.dtype
    h = _rmsnorm(x, norm_g)
    z, xbc, dt_raw, cq, ckv, kpe, rq, rk, rv, rg = _split(h @ w_in, IN_WIDTHS)
    full = jnp.concatenate([conv_buf.astype(dtype), xbc], axis=1)
    new_conv = full[:, full.shape[1] - (CONV_WIDTH - 1):]
    xbc = jax.nn.silu(_causal_dwconv(full, conv_w, conv_b))
    xm, bm, cm = _split(xbc, (M_INNER, M_GROUPS * M_STATE, M_GROUPS * M_STATE))
    dt = jax.nn.softplus(dt_raw.astype(F32) + dt_bias.astype(F32))
    a = -jnp.exp(a_log.astype(F32))
    xm = xm.reshape(bsz, L, M_HEADS, M_HEAD_DIM)
    y, new_ssm = _ssd(xm, dt, a, bm.reshape(bsz, L, M_GROUPS, M_STATE),
                      cm.reshape(bsz, L, M_GROUPS, M_STATE), ssm_h)
    y = y + d_skip.astype(F32)[:, None] * xm.astype(F32)
    y_m = _gated_group_rmsnorm(y.reshape(bsz, L, M_INNER), z, m_norm_g, M_GROUPS).astype(dtype)
    cq = _rmsnorm(cq, q_norm_g)
    q = (cq @ w_uq).reshape(bsz, L, A_HEADS, QK_NOPE_DIM + QK_ROPE_DIM)
    q_nope = q[..., :QK_NOPE_DIM]
    q_pe = _rope(q[..., QK_NOPE_DIM:], pos)
    ckv = _rmsnorm(ckv, kv_norm_g)
    kpe = _rope(kpe[:, :, None, :], pos)[:, :, 0, :]
    if paged is None:
        y_a = _mla_prompt(q_nope, q_pe, ckv, kpe, w_ukv)
    else:
        y_a = _mla_sample(q_nope, q_pe, ckv, kpe, w_ukv, *paged)
    rq = _rope(rq.reshape(bsz, L, R_HEADS, R_DK), pos)
    rk = _rope(rk.reshape(bsz, L, R_HEADS, R_DK), pos) * (R_DK ** -0.5)
    o, new_ret = _retention(rq, rk, rv.reshape(bsz, L, R_HEADS, R_DV), ret_s)
    y_r = (jax.nn.silu(rg.astype(F32)) * _head_groupnorm(o, r_norm_g)).astype(dtype)
    gates = jax.nn.sigmoid((h @ w_gate + b_gate).astype(F32)).astype(dtype)
    g_m, g_a, g_r = jnp.split(gates, N_BRANCH, axis=-1)
    mix = g_m * (y_m @ w_br_m) + g_a * (y_a @ w_br_a) + g_r * (y_r @ w_br_r)
    return (x + mix @ w_out, new_conv, new_ssm.astype(ssm_h.dtype), new_ret.astype(ret_s.dtype), ckv, kpe)


def _ffn_block(x, g, w_g, w_u, w_d):
    h = _rmsnorm(x, g)
    return x + (jax.nn.silu(h @ w_g) * (h @ w_u)) @ w_d


def setup_inputs(seed: int = 0) -> dict:
    key = jax.random.key(seed)
    keys = list(jax.random.split(key, 40))
    ctr = [0]

    def nk():
        ctr[0] += 1
        return keys[ctr[0] - 1]

    def nrm(shape, scale):
        return jax.random.normal(nk(), shape, F32) * scale

    def gain(shape):
        return 1.0 + 0.02 * jax.random.normal(nk(), shape, F32)

    n_pages = PAST_LEN // PAGE_SIZE
    n_used = DEC_BATCH * n_pages
    n_pool = n_used + max(1, n_used // 4)
    page_table = jax.random.permutation(nk(), n_pool)[:n_used].reshape(DEC_BATCH, n_pages).astype(jnp.int32)
    dt0 = jnp.exp(jax.random.uniform(nk(), (DEPTH, M_HEADS), F32, float(np.log(1e-3)), float(np.log(1e-1))))
    return {
        'x_prompt': nrm((BATCH, SEQ, D_MODEL), 1.0),
        'x_sample': nrm((DEC_BATCH, DEC_SEQ, D_MODEL), 1.0),
        'cache_ckv': nrm((DEPTH, n_pool, PAGE_SIZE, KV_LORA_RANK), 1.0),
        'cache_kpe': nrm((DEPTH, n_pool, PAGE_SIZE, QK_ROPE_DIM), 1.0),
        'state_conv': nrm((DEPTH, DEC_BATCH, CONV_WIDTH - 1, CONV_DIM), 1.0),
        'state_ssm': nrm((DEPTH, DEC_BATCH, M_HEADS, M_HEAD_DIM, M_STATE), 0.5),
        'state_ret': nrm((DEPTH, DEC_BATCH, R_HEADS, R_DK, R_DV), 0.5),
        'page_table': page_table,
        'norm_mix_g': gain((DEPTH, D_MODEL)),
        'w_in': nrm((DEPTH, D_MODEL, D_IN), D_MODEL ** -0.5),
        'conv_w': nrm((DEPTH, CONV_WIDTH, CONV_DIM), CONV_WIDTH ** -0.5),
        'conv_b': nrm((DEPTH, CONV_DIM), 0.02),
        'dt_bias': dt0 + jnp.log(-jnp.expm1(-dt0)),
        'a_log': jnp.log(jax.random.uniform(nk(), (DEPTH, M_HEADS), F32, 1.0, 16.0)),
        'd_skip': gain((DEPTH, M_HEADS)),
        'm_norm_g': gain((DEPTH, M_INNER)),
        'q_norm_g': gain((DEPTH, Q_LORA_RANK)),
        'w_uq': nrm((DEPTH, Q_LORA_RANK, A_HEADS * (QK_NOPE_DIM + QK_ROPE_DIM)), Q_LORA_RANK ** -0.5),
        'kv_norm_g': gain((DEPTH, KV_LORA_RANK)),
        'w_ukv': nrm((DEPTH, KV_LORA_RANK, A_HEADS * (QK_NOPE_DIM + V_HEAD_DIM)), KV_LORA_RANK ** -0.5),
        'r_norm_g': gain((DEPTH, R_V_WIDTH)),
        'w_br_m': nrm((DEPTH, M_INNER, D_MODEL), M_INNER ** -0.5),
        'w_br_a': nrm((DEPTH, A_WIDTH, D_MODEL), A_WIDTH ** -0.5),
        'w_br_r': nrm((DEPTH, R_V_WIDTH, D_MODEL), R_V_WIDTH ** -0.5),
        'w_gate': nrm((DEPTH, D_MODEL, N_BRANCH * D_MODEL), D_MODEL ** -0.5),
        'b_gate': nrm((DEPTH, N_BRANCH * D_MODEL), 0.1),
        'w_out': nrm((DEPTH, D_MODEL, D_MODEL), D_MODEL ** -0.5),
        'norm_ffn_g': gain((DEPTH, D_MODEL)),
        'w_ffn_gate': nrm((DEPTH, D_MODEL, D_FF), D_MODEL ** -0.5),
        'w_ffn_up': nrm((DEPTH, D_MODEL, D_FF), D_MODEL ** -0.5),
        'w_ffn_down': nrm((DEPTH, D_FF, D_MODEL), D_FF ** -0.5),
        'norm_final_g': gain((D_MODEL,)),
    }


def reference(x_prompt, x_sample, cache_ckv, cache_kpe, state_conv, state_ssm, state_ret, page_table,
              norm_mix_g, w_in, conv_w, conv_b, dt_bias, a_log, d_skip, m_norm_g,
              q_norm_g, w_uq, kv_norm_g, w_ukv, r_norm_g,
              w_br_m, w_br_a, w_br_r, w_gate, b_gate, w_out,
              norm_ffn_g, w_ffn_gate, w_ffn_up, w_ffn_down, norm_final_g):
    bp, lp, _ = x_prompt.shape
    ls = x_sample.shape[1]
    pos_p = jnp.arange(lp, dtype=jnp.int32)
    pos_s = PAST_LEN + jnp.arange(ls, dtype=jnp.int32)
    xp, xs = x_prompt, x_sample
    p_ckv, p_kpe, p_conv, p_ssm, p_ret = [], [], [], [], []
    s_ckv, s_kpe, s_conv, s_ssm, s_ret = [], [], [], [], []
    for l in range(DEPTH):
        lw = (norm_mix_g[l], w_in[l], conv_w[l], conv_b[l], dt_bias[l], a_log[l], d_skip[l], m_norm_g[l],
              q_norm_g[l], w_uq[l], kv_norm_g[l], w_ukv[l], r_norm_g[l],
              w_br_m[l], w_br_a[l], w_br_r[l], w_gate[l], b_gate[l], w_out[l])
        zc = jnp.zeros((bp, CONV_WIDTH - 1, CONV_DIM), xp.dtype)
        zs = jnp.zeros((bp, M_HEADS, M_HEAD_DIM, M_STATE), xp.dtype)
        zr = jnp.zeros((bp, R_HEADS, R_DK, R_DV), xp.dtype)
        xp, c1, h1, r1, k1, e1 = _mixer_block(xp, pos_p, zc, zs, zr, None, *lw)
        xs, c2, h2, r2, k2, e2 = _mixer_block(xs, pos_s, state_conv[l], state_ssm[l], state_ret[l],
                                              (cache_ckv, cache_kpe, page_table, l), *lw)
        xp = _ffn_block(xp, norm_ffn_g[l], w_ffn_gate[l], w_ffn_up[l], w_ffn_down[l])
        xs = _ffn_block(xs, norm_ffn_g[l], w_ffn_gate[l], w_ffn_up[l], w_ffn_down[l])
        p_ckv.append(k1); p_kpe.append(e1); p_conv.append(c1); p_ssm.append(h1); p_ret.append(r1)
        s_ckv.append(k2); s_kpe.append(e2); s_conv.append(c2); s_ssm.append(h2); s_ret.append(r2)
    y_prompt = _rmsnorm(xp, norm_final_g)
    y_sample = _rmsnorm(xs, norm_final_g)
    return (y_prompt, y_sample,
            jnp.stack(p_ckv), jnp.stack(p_kpe), jnp.stack(p_conv), jnp.stack(p_ssm), jnp.stack(p_ret),
            jnp.stack(s_ckv), jnp.stack(s_kpe), jnp.stack(s_conv), jnp.stack(s_ssm), jnp.stack(s_ret))
```

```python
import functools

import jax
import jax.numpy as jnp
import numpy as np
from jax import lax
from jax.experimental import pallas as pl
from jax.experimental.pallas import tpu as pltpu

D_MODEL = 4096
DEPTH = 2
PAST_LEN = 16384
PAGE_SIZE = 128
M_HEAD_DIM = 64
M_HEADS = D_MODEL // 128
M_INNER = M_HEADS * M_HEAD_DIM
M_GROUPS = 4
M_STATE = 128
CONV_WIDTH = 4
CONV_DIM = M_INNER + 2 * M_GROUPS * M_STATE
CHUNK = 128
A_HEADS = D_MODEL // 256
QK_NOPE_DIM = 128
QK_ROPE_DIM = 64
V_HEAD_DIM = 128
Q_LORA_RANK = 3 * D_MODEL // 16
KV_LORA_RANK = D_MODEL // 16
A_WIDTH = A_HEADS * V_HEAD_DIM
Q_BLOCK = 128
R_HEADS = D_MODEL // 512
R_DK = 128
R_DV = 256
R_QK_WIDTH = R_HEADS * R_DK
R_V_WIDTH = R_HEADS * R_DV
D_FF = -(-8 * D_MODEL // (3 * 256)) * 256
ROPE_BASE = 10000.0
EPS = 1e-6
GN_EPS = 1e-5
N_BRANCH = 3
IN_WIDTHS = (M_INNER, CONV_DIM, M_HEADS, Q_LORA_RANK, KV_LORA_RANK, QK_ROPE_DIM,
             R_QK_WIDTH, R_QK_WIDTH, R_V_WIDTH, R_V_WIDTH)
D_IN = sum(IN_WIDTHS)

F32 = jnp.float32
BF16 = jnp.bfloat16

V7X_VMEM_LIMIT_BYTES = 56 * 1024 * 1024


def _round_up(n, m):
    return -(-n // m) * m


def _mm_kernel(x_ref, w_ref, o_ref):
    o_ref[...] = jnp.dot(x_ref[...], w_ref[...], preferred_element_type=F32).astype(o_ref.dtype)


def _mm_kgrid_kernel(x_ref, w_ref, o_ref, acc_ref):
    k = pl.program_id(2)

    @pl.when(k == 0)
    def _():
        acc_ref[...] = jnp.zeros_like(acc_ref)

    acc_ref[...] += jnp.dot(x_ref[...], w_ref[...], preferred_element_type=F32)

    @pl.when(k == pl.num_programs(2) - 1)
    def _():
        o_ref[...] = acc_ref[...].astype(o_ref.dtype)


def _pick_tile(n, prefs):
    for t in prefs:
        if n % t == 0:
            return t
    return n


def _mm(x, w, out_dtype=F32, tk=None):
    m, k = x.shape
    n = w.shape[1]
    tm = _pick_tile(m, (1024, 512, 256, 128))
    tn = _pick_tile(n, (1024, 896, 512, 256, 128))
    if tk is None:
        return pl.pallas_call(
            _mm_kernel,
            grid=(m // tm, n // tn),
            in_specs=[pl.BlockSpec((tm, k), lambda i, j: (i, 0)),
                      pl.BlockSpec((k, tn), lambda i, j: (0, j))],
            out_specs=pl.BlockSpec((tm, tn), lambda i, j: (i, j)),
            out_shape=jax.ShapeDtypeStruct((m, n), out_dtype),
            compiler_params=pltpu.CompilerParams(
                dimension_semantics=("parallel", "parallel"),
                vmem_limit_bytes=V7X_VMEM_LIMIT_BYTES),
            name="mm",
        )(x, w)
    return pl.pallas_call(
        _mm_kgrid_kernel,
        grid=(m // tm, n // tn, k // tk),
        in_specs=[pl.BlockSpec((tm, tk), lambda i, j, kk: (i, kk)),
                  pl.BlockSpec((tk, tn), lambda i, j, kk: (kk, j))],
        out_specs=pl.BlockSpec((tm, tn), lambda i, j, kk: (i, j)),
        out_shape=jax.ShapeDtypeStruct((m, n), out_dtype),
        scratch_shapes=[pltpu.VMEM((tm, tn), F32)],
        compiler_params=pltpu.CompilerParams(
            dimension_semantics=("parallel", "parallel", "arbitrary"),
            vmem_limit_bytes=V7X_VMEM_LIMIT_BYTES),
        name="mm_kgrid",
    )(x, w)


def _dense(x, w_bf16, n_out=None, tk=None):
    lead = x.shape[:-1]
    y = _mm(x.reshape(-1, x.shape[-1]).astype(BF16), w_bf16, F32, tk)
    if n_out is not None and n_out != y.shape[-1]:
        y = y[:, :n_out]
    return y.reshape(lead + (y.shape[-1],))


def _split(x, widths):
    idx = np.cumsum(widths)[:-1].tolist()
    return jnp.split(x, idx, axis=-1)


def _chunk_len(n, block=CHUNK):
    return block if n % block == 0 else n


def _rmsnorm(x, g):
    x32 = x.astype(F32)
    y = x32 * lax.rsqrt(jnp.mean(x32 * x32, axis=-1, keepdims=True) + EPS)
    return (y * g.astype(F32)).astype(x.dtype)


def _gated_group_rmsnorm(y, z, g, n_groups):
    v = y.astype(F32) * jax.nn.silu(z.astype(F32))
    shp = v.shape
    v = v.reshape(shp[:-1] + (n_groups, shp[-1] // n_groups))
    v = v * lax.rsqrt(jnp.mean(v * v, axis=-1, keepdims=True) + EPS)
    return v.reshape(shp) * g.astype(F32)


def _head_groupnorm(o, g):
    mu = jnp.mean(o, axis=-1, keepdims=True)
    var = jnp.mean(jnp.square(o - mu), axis=-1, keepdims=True)
    o = (o - mu) * lax.rsqrt(var + GN_EPS)
    return o.reshape(o.shape[:2] + (-1,)) * g.astype(F32)


def _rope(x, pos):
    d = x.shape[-1]
    half = d // 2
    inv = ROPE_BASE ** (-2.0 * jnp.arange(half, dtype=F32) / d)
    ang = pos.astype(F32)[:, None] * inv[None, :]
    cos = jnp.cos(ang)[None, :, None, :]
    sin = jnp.sin(ang)[None, :, None, :]
    x32 = x.astype(F32)
    x1, x2 = x32[..., :half], x32[..., half:]
    return jnp.concatenate([x1 * cos - x2 * sin, x2 * cos + x1 * sin], axis=-1).astype(x.dtype)


def _causal_dwconv(full, w, b):
    out = lax.conv_general_dilated(full, w[:, None, :].astype(full.dtype), window_strides=(1,),
                                   padding='VALID', dimension_numbers=('NWC', 'WIO', 'NWC'),
                                   feature_group_count=full.shape[-1])
    return out + b.astype(full.dtype)


def _ssd(x, dt, a, b_in, c_in, h0):
    bsz, L, H, P = x.shape
    G, N = b_in.shape[-2:]
    E = H // G
    Q = _chunk_len(L)
    C = L // Q
    xd = (x.astype(F32) * dt[..., None]).reshape(bsz, C, Q, G, E, P)
    la = (dt * a).reshape(bsz, C, Q, G, E)
    bc = b_in.astype(F32).reshape(bsz, C, Q, G, N)
    cc = c_in.astype(F32).reshape(bsz, C, Q, G, N)
    cum = jnp.cumsum(la, axis=2)
    causal = jnp.tril(jnp.ones((Q, Q), bool))[None, None, :, :, None, None]
    seg = jnp.exp(jnp.where(causal, cum[:, :, :, None] - cum[:, :, None, :], -jnp.inf))
    cb = jnp.einsum('bclgn,bcsgn->bclsg', cc, bc)
    y_diag = jnp.einsum('bclsge,bcsgep->bclgep', cb[..., None] * seg, xd)
    xw = xd * jnp.exp(cum[:, :, -1:] - cum)[..., None]
    chunk_states = jnp.einsum('bclgn,bclgep->bcgepn', bc, xw)
    chunk_decay = jnp.exp(cum[:, :, -1])

    def step(h, inp):
        st, dec = inp
        return h * dec[..., None, None] + st, h

    h_last, h_prev = lax.scan(step, h0.astype(F32).reshape(bsz, G, E, P, N),
                              (jnp.moveaxis(chunk_states, 1, 0), jnp.moveaxis(chunk_decay, 1, 0)))
    h_prev = jnp.moveaxis(h_prev, 0, 1)
    y_off = jnp.einsum('bclgn,bcgepn->bclgep', cc, h_prev) * jnp.exp(cum)[..., None]
    y = (y_diag + y_off).reshape(bsz, L, H, P)
    return y, h_last.reshape(bsz, H, P, N)


def _retention(q, k, v, s0):
    bsz, L, H, DK = q.shape
    DV = v.shape[-1]
    Q = _chunk_len(L)
    C = L // Q
    log_g = jnp.log1p(-jnp.exp2(-5.0 - jnp.arange(H, dtype=F32)))
    qc = q.astype(F32).reshape(bsz, C, Q, H, DK)
    kc = k.astype(F32).reshape(bsz, C, Q, H, DK)
    vc = v.astype(F32).reshape(bsz, C, Q, H, DV)
    t = jnp.arange(Q, dtype=F32)
    rel = t[:, None] - t[None, :]
    dmat = jnp.where((rel >= 0)[..., None], jnp.exp(jnp.maximum(rel, 0.0)[..., None] * log_g), 0.0)
    scores = jnp.einsum('bclhd,bcshd->bclsh', qc, kc) * dmat
    o_intra = jnp.einsum('bclsh,bcshe->bclhe', scores, vc)
    k_w = kc * jnp.exp((Q - 1 - t)[:, None] * log_g)[..., None]
    chunk_states = jnp.einsum('bcshd,bcshe->bchde', k_w, vc)
    chunk_decay = jnp.exp(Q * log_g)

    def step(s, st):
        return s * chunk_decay[:, None, None] + st, s

    s_last, s_prev = lax.scan(step, s0.astype(F32), jnp.moveaxis(chunk_states, 1, 0))
    s_prev = jnp.moveaxis(s_prev, 0, 1)
    o_cross = jnp.einsum('bclhd,bchde->bclhe', qc, s_prev) * jnp.exp((t + 1.0)[:, None] * log_g)[..., None]
    return (o_intra + o_cross).reshape(bsz, L, H, DV), s_last


def _mla_prompt(q_nope, q_pe, ckv, kpe, w_ukv_bf16):
    bsz, S, H, _ = q_nope.shape
    kv = _dense(ckv, w_ukv_bf16).reshape(bsz, S, H, QK_NOPE_DIM + V_HEAD_DIM)
    k_nope, v = kv[..., :QK_NOPE_DIM], kv[..., QK_NOPE_DIM:]
    qb = _chunk_len(S, Q_BLOCK)
    nb = S // qb
    qn = jnp.moveaxis(q_nope.reshape(bsz, nb, qb, H, QK_NOPE_DIM), 1, 0)
    qp = jnp.moveaxis(q_pe.reshape(bsz, nb, qb, H, QK_ROPE_DIM), 1, 0)
    kpos = jnp.arange(S)
    scale = (QK_NOPE_DIM + QK_ROPE_DIM) ** -0.5

    def block(args):
        qn_b, qp_b, i = args
        s = jnp.einsum('bqhd,bkhd->bhqk', qn_b, k_nope) + jnp.einsum('bqhd,bkd->bhqk', qp_b, kpe)
        qpos = i * qb + jnp.arange(qb)
        s = jnp.where(kpos[None, :] <= qpos[:, None], s.astype(F32) * scale, -jnp.inf)
        p = jax.nn.softmax(s, axis=-1).astype(v.dtype)
        return jnp.einsum('bhqk,bkhd->bqhd', p, v)

    o = lax.map(block, (qn, qp, jnp.arange(nb)))
    return jnp.moveaxis(o, 0, 1).reshape(bsz, S, H * V_HEAD_DIM)


def _mla_sample(q_nope, q_pe, ckv, kpe, w_ukv, cache_ckv, cache_kpe, page_table, layer):
    bsz, Lq, H, _ = q_nope.shape
    ckv_past = cache_ckv[layer, page_table].reshape(bsz, -1, KV_LORA_RANK)
    kpe_past = cache_kpe[layer, page_table].reshape(bsz, -1, QK_ROPE_DIM)
    n_past = ckv_past.shape[1]
    scale = (QK_NOPE_DIM + QK_ROPE_DIM) ** -0.5
    w = w_ukv.reshape(KV_LORA_RANK, H, QK_NOPE_DIM + V_HEAD_DIM)
    w_uk, w_uv = w[..., :QK_NOPE_DIM], w[..., QK_NOPE_DIM:]
    q_lat = jnp.einsum('bqhd,rhd->bqhr', q_nope, w_uk)
    s_past = jnp.einsum('bqhr,bkr->bhqk', q_lat, ckv_past) + jnp.einsum('bqhd,bkd->bhqk', q_pe, kpe_past)
    s_new = jnp.einsum('bqhr,bkr->bhqk', q_lat, ckv) + jnp.einsum('bqhd,bkd->bhqk', q_pe, kpe)
    causal = jnp.tril(jnp.ones((Lq, Lq), bool))
    s_new = jnp.where(causal, s_new.astype(F32) * scale, -jnp.inf)
    s = jnp.concatenate([s_past.astype(F32) * scale, s_new], axis=-1)
    p = jax.nn.softmax(s, axis=-1).astype(ckv.dtype)
    o_lat = (jnp.einsum('bhqk,bkr->bqhr', p[..., :n_past], ckv_past)
             + jnp.einsum('bhqk,bkr->bqhr', p[..., n_past:], ckv))
    o = jnp.einsum('bqhr,rhd->bqhd', o_lat, w_uv)
    return o.reshape(bsz, Lq, H * V_HEAD_DIM)


def _mixer_block(x, pos, conv_buf, ssm_h, ret_s, paged, lw):
    bsz, L, _ = x.shape
    dtype = x.dtype
    h = _rmsnorm(x, lw['norm_g'])
    proj = _dense(h, lw['w_in'], D_IN)
    z, xbc, dt_raw, cq, ckv, kpe, rq, rk, rv, rg = _split(proj, IN_WIDTHS)
    full = jnp.concatenate([conv_buf.astype(dtype), xbc], axis=1)
    new_conv = full[:, full.shape[1] - (CONV_WIDTH - 1):]
    xbc = jax.nn.silu(_causal_dwconv(full, lw['conv_w'], lw['conv_b']))
    xm, bm, cm = _split(xbc, (M_INNER, M_GROUPS * M_STATE, M_GROUPS * M_STATE))
    dt = jax.nn.softplus(dt_raw.astype(F32) + lw['dt_bias'].astype(F32))
    a = -jnp.exp(lw['a_log'].astype(F32))
    xm = xm.reshape(bsz, L, M_HEADS, M_HEAD_DIM)
    y, new_ssm = _ssd(xm, dt, a, bm.reshape(bsz, L, M_GROUPS, M_STATE),
                      cm.reshape(bsz, L, M_GROUPS, M_STATE), ssm_h)
    y = y + lw['d_skip'].astype(F32)[:, None] * xm.astype(F32)
    y_m = _gated_group_rmsnorm(y.reshape(bsz, L, M_INNER), z, lw['m_norm_g'], M_GROUPS).astype(dtype)
    cq = _rmsnorm(cq, lw['q_norm_g'])
    q = _dense(cq, lw['w_uq']).reshape(bsz, L, A_HEADS, QK_NOPE_DIM + QK_ROPE_DIM)
    q_nope = q[..., :QK_NOPE_DIM]
    q_pe = _rope(q[..., QK_NOPE_DIM:], pos)
    ckv = _rmsnorm(ckv, lw['kv_norm_g'])
    kpe = _rope(kpe[:, :, None, :], pos)[:, :, 0, :]
    if paged is None:
        y_a = _mla_prompt(q_nope, q_pe, ckv, kpe, lw['w_ukv'])
    else:
        y_a = _mla_sample(q_nope, q_pe, ckv, kpe, lw['w_ukv_f32'], *paged)
    rq = _rope(rq.reshape(bsz, L, R_HEADS, R_DK), pos)
    rk = _rope(rk.reshape(bsz, L, R_HEADS, R_DK), pos) * (R_DK ** -0.5)
    o, new_ret = _retention(rq, rk, rv.reshape(bsz, L, R_HEADS, R_DV), ret_s)
    y_r = (jax.nn.silu(rg.astype(F32)) * _head_groupnorm(o, lw['r_norm_g'])).astype(dtype)
    gates = jax.nn.sigmoid((_dense(h, lw['w_gate']) + lw['b_gate']).astype(F32)).astype(dtype)
    g_m, g_a, g_r = jnp.split(gates, N_BRANCH, axis=-1)
    mix = (g_m * _dense(y_m, lw['w_br_m']) + g_a * _dense(y_a, lw['w_br_a'])
           + g_r * _dense(y_r, lw['w_br_r']))
    return (x + _dense(mix, lw['w_out']), new_conv, new_ssm.astype(ssm_h.dtype),
            new_ret.astype(ret_s.dtype), ckv, kpe)


def _ffn_block(x, lw):
    h = _rmsnorm(x, lw['norm_ffn_g'])
    a = jax.nn.silu(_dense(h, lw['w_ffn_gate'])) * _dense(h, lw['w_ffn_up'])
    return x + _dense(a, lw['w_ffn_down'], tk=lw['w_ffn_down'].shape[0] // 4)


def _pad_cols(w, n):
    return jnp.pad(w, ((0, 0), (0, n - w.shape[1])))


def kernel(x_prompt, x_sample, cache_ckv, cache_kpe, state_conv, state_ssm, state_ret, page_table,
           norm_mix_g, w_in, conv_w, conv_b, dt_bias, a_log, d_skip, m_norm_g,
           q_norm_g, w_uq, kv_norm_g, w_ukv, r_norm_g,
           w_br_m, w_br_a, w_br_r, w_gate, b_gate, w_out,
           norm_ffn_g, w_ffn_gate, w_ffn_up, w_ffn_down, norm_final_g):
    bp, lp, _ = x_prompt.shape
    ls = x_sample.shape[1]
    pos_p = jnp.arange(lp, dtype=jnp.int32)
    pos_s = PAST_LEN + jnp.arange(ls, dtype=jnp.int32)
    xp, xs = x_prompt, x_sample
    d_in_pad = _round_up(D_IN, 896)
    d_ff_pad = _round_up(D_FF, 1024)
    outs_p = [[] for _ in range(5)]
    outs_s = [[] for _ in range(5)]
    for l in range(DEPTH):
        lw = dict(
            norm_g=norm_mix_g[l], w_in=_pad_cols(w_in[l].astype(BF16), d_in_pad),
            conv_w=conv_w[l], conv_b=conv_b[l], dt_bias=dt_bias[l], a_log=a_log[l],
            d_skip=d_skip[l], m_norm_g=m_norm_g[l], q_norm_g=q_norm_g[l],
            w_uq=w_uq[l].astype(BF16), kv_norm_g=kv_norm_g[l], w_ukv=w_ukv[l].astype(BF16),
            w_ukv_f32=w_ukv[l], r_norm_g=r_norm_g[l],
            w_br_m=w_br_m[l].astype(BF16), w_br_a=w_br_a[l].astype(BF16),
            w_br_r=w_br_r[l].astype(BF16), w_gate=w_gate[l].astype(BF16), b_gate=b_gate[l],
            w_out=w_out[l].astype(BF16), norm_ffn_g=norm_ffn_g[l],
            w_ffn_gate=_pad_cols(w_ffn_gate[l].astype(BF16), d_ff_pad),
            w_ffn_up=_pad_cols(w_ffn_up[l].astype(BF16), d_ff_pad),
            w_ffn_down=jnp.pad(w_ffn_down[l].astype(BF16), ((0, d_ff_pad - D_FF), (0, 0))),
        )
        zc = jnp.zeros((bp, CONV_WIDTH - 1, CONV_DIM), xp.dtype)
        zs = jnp.zeros((bp, M_HEADS, M_HEAD_DIM, M_STATE), xp.dtype)
        zr = jnp.zeros((bp, R_HEADS, R_DK, R_DV), xp.dtype)
        xp, c1, h1, r1, k1, e1 = _mixer_block(xp, pos_p, zc, zs, zr, None, lw)
        xs, c2, h2, r2, k2, e2 = _mixer_block(xs, pos_s, state_conv[l], state_ssm[l], state_ret[l],
                                              (cache_ckv, cache_kpe, page_table, l), lw)
        xp = _ffn_block(xp, lw)
        xs = _ffn_block(xs, lw)
        for lst, v in zip(outs_p, (k1, e1, c1, h1, r1)):
            lst.append(v)
        for lst, v in zip(outs_s, (k2, e2, c2, h2, r2)):
            lst.append(v)
    y_prompt = _rmsnorm(xp, norm_final_g)
    y_sample = _rmsnorm(xs, norm_final_g)
    return ((y_prompt, y_sample) + tuple(jnp.stack(v) for v in outs_p)
            + tuple(jnp.stack(v) for v in outs_s))
```

```python
import functools

import jax
import jax.numpy as jnp
import numpy as np
from jax import lax
from jax.experimental import pallas as pl
from jax.experimental.pallas import tpu as pltpu

D_MODEL = 4096
DEPTH = 2
PAST_LEN = 16384
PAGE_SIZE = 128
M_HEAD_DIM = 64
M_HEADS = D_MODEL // 128
M_INNER = M_HEADS * M_HEAD_DIM
M_GROUPS = 4
M_STATE = 128
CONV_WIDTH = 4
CONV_DIM = M_INNER + 2 * M_GROUPS * M_STATE
CHUNK = 128
A_HEADS = D_MODEL // 256
QK_NOPE_DIM = 128
QK_ROPE_DIM = 64
V_HEAD_DIM = 128
Q_LORA_RANK = 3 * D_MODEL // 16
KV_LORA_RANK = D_MODEL // 16
A_WIDTH = A_HEADS * V_HEAD_DIM
R_HEADS = D_MODEL // 512
R_DK = 128
R_DV = 256
R_QK_WIDTH = R_HEADS * R_DK
R_V_WIDTH = R_HEADS * R_DV
D_FF = -(-8 * D_MODEL // (3 * 256)) * 256
ROPE_BASE = 10000.0
EPS = 1e-6
GN_EPS = 1e-5
N_BRANCH = 3
IN_WIDTHS = (M_INNER, CONV_DIM, M_HEADS, Q_LORA_RANK, KV_LORA_RANK, QK_ROPE_DIM,
             R_QK_WIDTH, R_QK_WIDTH, R_V_WIDTH, R_V_WIDTH)

F32 = jnp.float32
BF16 = jnp.bfloat16

V7X_LANES = 128
V7X_MXU_DIM = 256
V7X_VMEM_LIMIT_BYTES = 56 * 1024 * 1024

IN_WIDTHS_PAD = tuple(-(-w // V7X_LANES) * V7X_LANES for w in IN_WIDTHS)
IN_OFFSETS_PAD = tuple(int(v) for v in np.cumsum((0,) + IN_WIDTHS_PAD[:-1]))
D_IN_PAD = sum(IN_WIDTHS_PAD)
D_FF_PAD = -(-D_FF // 1024) * 1024
A_QK_PAD = V7X_MXU_DIM
MLA_SCALE = (QK_NOPE_DIM + QK_ROPE_DIM) ** -0.5
FLASH_BLOCK = 512
SAMPLE_PAGES_PER_CHUNK = 16


def _cparams(semantics):
    return pltpu.CompilerParams(dimension_semantics=semantics,
                                vmem_limit_bytes=V7X_VMEM_LIMIT_BYTES)


def _pick_tile(n, prefs):
    for t in prefs:
        if n % t == 0:
            return t
    return n


def _sigmoid(x):
    return 1.0 / (1.0 + jnp.exp(-x))


def _mm_kernel(x_ref, w_ref, o_ref):
    o_ref[...] = jnp.dot(x_ref[...], w_ref[...], preferred_element_type=F32).astype(o_ref.dtype)


def _mm_bias_sigmoid_kernel(x_ref, w_ref, b_ref, o_ref):
    acc = jnp.dot(x_ref[...], w_ref[...], preferred_element_type=F32)
    o_ref[...] = _sigmoid(acc + b_ref[...]).astype(o_ref.dtype)


def _mm_residual_kernel(x_ref, w_ref, r_ref, o_ref):
    o_ref[...] = r_ref[...] + jnp.dot(x_ref[...], w_ref[...], preferred_element_type=F32)


def _mm_swiglu_kernel(x_ref, wg_ref, wu_ref, o_ref):
    x = x_ref[...]
    g = jnp.dot(x, wg_ref[...], preferred_element_type=F32)
    u = jnp.dot(x, wu_ref[...], preferred_element_type=F32)
    o_ref[...] = (g * _sigmoid(g) * u).astype(o_ref.dtype)


def _mm_merge3_kernel(ym_ref, ya_ref, yr_ref, wm_ref, wa_ref, wr_ref, gm_ref, ga_ref, gr_ref, o_ref):
    mix = gm_ref[...].astype(F32) * jnp.dot(ym_ref[...], wm_ref[...], preferred_element_type=F32)
    mix += ga_ref[...].astype(F32) * jnp.dot(ya_ref[...], wa_ref[...], preferred_element_type=F32)
    mix += gr_ref[...].astype(F32) * jnp.dot(yr_ref[...], wr_ref[...], preferred_element_type=F32)
    o_ref[...] = mix.astype(o_ref.dtype)


def _mm_kgrid_residual_kernel(x_ref, w_ref, r_ref, o_ref, acc_ref):
    k = pl.program_id(2)

    @pl.when(k == 0)
    def _():
        acc_ref[...] = r_ref[...]

    acc_ref[...] += jnp.dot(x_ref[...], w_ref[...], preferred_element_type=F32)

    @pl.when(k == pl.num_programs(2) - 1)
    def _():
        o_ref[...] = acc_ref[...]


def _row_tile(m):
    return _pick_tile(m, (1024, 512, 256, 128))


def _mm(x, w, out_dtype=F32):
    m, k = x.shape
    n = w.shape[1]
    tm = _row_tile(m)
    tn = _pick_tile(n, (1024, 896, 512, 256, 128))
    return pl.pallas_call(
        _mm_kernel,
        grid=(m // tm, n // tn),
        in_specs=[pl.BlockSpec((tm, k), lambda i, j: (i, 0)),
                  pl.BlockSpec((k, tn), lambda i, j: (0, j))],
        out_specs=pl.BlockSpec((tm, tn), lambda i, j: (i, j)),
        out_shape=jax.ShapeDtypeStruct((m, n), out_dtype),
        compiler_params=_cparams(("parallel", "parallel")),
        name="mm",
    )(x, w)


def _mm_bias_sigmoid(x, w, b):
    m, k = x.shape
    n = w.shape[1]
    tm = _row_tile(m)
    tn = _pick_tile(n, (1024, 512, 256, 128))
    return pl.pallas_call(
        _mm_bias_sigmoid_kernel,
        grid=(m // tm, n // tn),
        in_specs=[pl.BlockSpec((tm, k), lambda i, j: (i, 0)),
                  pl.BlockSpec((k, tn), lambda i, j: (0, j)),
                  pl.BlockSpec((1, tn), lambda i, j: (0, j))],
        out_specs=pl.BlockSpec((tm, tn), lambda i, j: (i, j)),
        out_shape=jax.ShapeDtypeStruct((m, n), BF16),
        compiler_params=_cparams(("parallel", "parallel")),
        name="mm_gate",
    )(x, w, b.reshape(1, n).astype(F32))


def _mm_residual(x, w, r):
    m, k = x.shape
    n = w.shape[1]
    tm = _row_tile(m)
    tn = _pick_tile(n, (1024, 512, 256, 128))
    return pl.pallas_call(
        _mm_residual_kernel,
        grid=(m // tm, n // tn),
        in_specs=[pl.BlockSpec((tm, k), lambda i, j: (i, 0)),
                  pl.BlockSpec((k, tn), lambda i, j: (0, j)),
                  pl.BlockSpec((tm, tn), lambda i, j: (i, j))],
        out_specs=pl.BlockSpec((tm, tn), lambda i, j: (i, j)),
        out_shape=jax.ShapeDtypeStruct((m, n), F32),
        compiler_params=_cparams(("parallel", "parallel")),
        name="mm_residual",
    )(x, w, r)


def _mm_swiglu(x, wg, wu):
    m, k = x.shape
    n = wg.shape[1]
    tm = _row_tile(m)
    tn = _pick_tile(n, (512, 256, 128))
    return pl.pallas_call(
        _mm_swiglu_kernel,
        grid=(m // tm, n // tn),
        in_specs=[pl.BlockSpec((tm, k), lambda i, j: (i, 0)),
                  pl.BlockSpec((k, tn), lambda i, j: (0, j)),
                  pl.BlockSpec((k, tn), lambda i, j: (0, j))],
        out_specs=pl.BlockSpec((tm, tn), lambda i, j: (i, j)),
        out_shape=jax.ShapeDtypeStruct((m, n), BF16),
        compiler_params=_cparams(("parallel", "parallel")),
        name="mm_swiglu",
    )(x, wg, wu)


def _mm_merge3(ym, ya, yr, wm, wa, wr, gates):
    m, k = ym.shape
    n = wm.shape[1]
    tm = _row_tile(m)
    tn = _pick_tile(n, (512, 256, 128))
    nb = n // tn
    y_spec = pl.BlockSpec((tm, k), lambda i, j: (i, 0))
    w_spec = pl.BlockSpec((k, tn), lambda i, j: (0, j))
    return pl.pallas_call(
        _mm_merge3_kernel,
        grid=(m // tm, nb),
        in_specs=[y_spec, y_spec, y_spec, w_spec, w_spec, w_spec,
                  pl.BlockSpec((tm, tn), lambda i, j: (i, j)),
                  pl.BlockSpec((tm, tn), lambda i, j: (i, j + nb)),
                  pl.BlockSpec((tm, tn), lambda i, j: (i, j + 2 * nb))],
        out_specs=pl.BlockSpec((tm, tn), lambda i, j: (i, j)),
        out_shape=jax.ShapeDtypeStruct((m, n), BF16),
        compiler_params=_cparams(("parallel", "parallel")),
        name="mm_merge3",
    )(ym, ya, yr, wm, wa, wr, gates, gates, gates)


def _mm_kgrid_residual(x, w, r, n_k):
    m, k = x.shape
    n = w.shape[1]
    tm = _row_tile(m)
    tn = _pick_tile(n, (1024, 512, 256, 128))
    tk = k // n_k
    return pl.pallas_call(
        _mm_kgrid_residual_kernel,
        grid=(m // tm, n // tn, n_k),
        in_specs=[pl.BlockSpec((tm, tk), lambda i, j, kk: (i, kk)),
                  pl.BlockSpec((tk, tn), lambda i, j, kk: (kk, j)),
                  pl.BlockSpec((tm, tn), lambda i, j, kk: (i, j))],
        out_specs=pl.BlockSpec((tm, tn), lambda i, j, kk: (i, j)),
        out_shape=jax.ShapeDtypeStruct((m, n), F32),
        scratch_shapes=[pltpu.VMEM((tm, tn), F32)],
        compiler_params=_cparams(("parallel", "parallel", "arbitrary")),
        name="mm_kgrid_residual",
    )(x, w, r)


def _rmsnorm_kernel(x_ref, g_ref, o_ref):
    x = x_ref[...]
    y = x * lax.rsqrt(jnp.mean(x * x, axis=-1, keepdims=True) + EPS)
    o_ref[...] = (y * g_ref[...]).astype(o_ref.dtype)


def _rmsnorm_rows(x, g, out_dtype):
    m, d = x.shape
    tr = _pick_tile(m, (256, 128))
    return pl.pallas_call(
        _rmsnorm_kernel,
        grid=(m // tr,),
        in_specs=[pl.BlockSpec((tr, d), lambda i: (i, 0)),
                  pl.BlockSpec((1, d), lambda i: (0, 0))],
        out_specs=pl.BlockSpec((tr, d), lambda i: (i, 0)),
        out_shape=jax.ShapeDtypeStruct((m, d), out_dtype),
        compiler_params=_cparams(("parallel",)),
        name="rmsnorm",
    )(x, g.reshape(1, d).astype(F32))


def _flash_kernel(q_ref, k_ref, v_ref, o_ref, *, blk):
    n_blocks = q_ref.shape[0] // blk
    dv = v_ref.shape[1]

    def kv_step(j, carry, q, masked):
        m, l, acc = carry
        c0 = pl.multiple_of(j * blk, blk)
        k = k_ref[pl.ds(c0, blk), :]
        v = v_ref[pl.ds(c0, blk), :]
        s = lax.dot_general(q, k, (((1,), (1,)), ((), ())), preferred_element_type=F32) * MLA_SCALE
        if masked:
            row = lax.broadcasted_iota(jnp.int32, (blk, blk), 0)
            col = lax.broadcasted_iota(jnp.int32, (blk, blk), 1)
            s = jnp.where(col <= row, s, -jnp.inf)
        m_new = jnp.maximum(m, jnp.max(s, axis=-1, keepdims=True))
        alpha = jnp.exp(m - m_new)
        p = jnp.exp(s - m_new)
        l = alpha * l + jnp.sum(p, axis=-1, keepdims=True)
        acc = alpha * acc + jnp.dot(p.astype(BF16), v, preferred_element_type=F32)
        return m_new, l, acc

    def q_block(i, _):
        r0 = pl.multiple_of(i * blk, blk)
        q = q_ref[pl.ds(r0, blk), :]
        init = (jnp.full((blk, 1), -jnp.inf, F32), jnp.zeros((blk, 1), F32), jnp.zeros((blk, dv), F32))
        carry = lax.fori_loop(0, i, lambda j, c: kv_step(j, c, q, False), init)
        _, l, acc = kv_step(i, carry, q, True)
        o_ref[pl.ds(r0, blk), :] = acc / l
        return 0

    lax.fori_loop(0, n_blocks, q_block, 0)


def _mla_prompt_attention(q_cat, k_cat, v, bsz, seq):
    n_heads = v.shape[1] // V_HEAD_DIM
    return pl.pallas_call(
        functools.partial(_flash_kernel, blk=FLASH_BLOCK),
        grid=(bsz, n_heads),
        in_specs=[pl.BlockSpec((seq, A_QK_PAD), lambda b, h: (b, h)),
                  pl.BlockSpec((seq, A_QK_PAD), lambda b, h: (b, h)),
                  pl.BlockSpec((seq, V_HEAD_DIM), lambda b, h: (b, h))],
        out_specs=pl.BlockSpec((seq, V_HEAD_DIM), lambda b, h: (b, h)),
        out_shape=jax.ShapeDtypeStruct((bsz * seq, n_heads * V_HEAD_DIM), F32),
        compiler_params=_cparams(("parallel", "parallel")),
        name="mla_prompt_flash",
    )(q_cat, k_cat, v)


def _paged_kernel(pt_ref, qlat_ref, qpe_ref, ckvn_ref, kpen_ref, ckv_hbm, kpe_hbm, o_ref,
                  ckv_buf, kpe_buf, sem, *, layer, ppc):
    n_samples, n_pages = pt_ref.shape
    n_chunks = n_pages // ppc
    t = ppc * PAGE_SIZE

    def page_copies(page_of, slot):
        for p in range(ppc):
            page = page_of(p)
            yield pltpu.make_async_copy(ckv_hbm.at[layer, page], ckv_buf.at[slot, p], sem.at[0, slot])
            yield pltpu.make_async_copy(kpe_hbm.at[layer, page], kpe_buf.at[slot, p], sem.at[1, slot])

    def start_chunk(b, c, slot):
        for cp in page_copies(lambda p: pt_ref[b, c * ppc + p], slot):
            cp.start()

    def wait_chunk(slot):
        for cp in page_copies(lambda p: 0, slot):
            cp.wait()

    start_chunk(0, 0, 0)

    def sample_body(b, _):
        q = qlat_ref[b]
        qp = qpe_ref[b]
        ck_new = ckvn_ref[pl.ds(b, 1), :]
        kp_new = kpen_ref[pl.ds(b, 1), :]
        s_new = (jnp.sum(q * ck_new, axis=-1, keepdims=True)
                 + jnp.sum(qp * kp_new, axis=-1, keepdims=True)) * MLA_SCALE
        q_bf = q.astype(BF16)
        qp_bf = qp.astype(BF16)
        init = (s_new, jnp.ones_like(s_new), jnp.broadcast_to(ck_new, q.shape))

        def chunk_body(c, carry):
            m, l, acc = carry
            slot = lax.rem(c, 2)
            c1 = c + 1
            wrap = c1 == n_chunks
            nb = b + jnp.where(wrap, 1, 0)
            nc = jnp.where(wrap, 0, c1)

            @pl.when(nb < n_samples)
            def _():
                start_chunk(nb, nc, 1 - slot)

            wait_chunk(slot)
            kv = ckv_buf[slot].reshape(t, KV_LORA_RANK).astype(BF16)
            kp = kpe_buf[slot].reshape(t, QK_ROPE_DIM).astype(BF16)
            dn = (((1,), (1,)), ((), ()))
            s = (lax.dot_general(q_bf, kv, dn, preferred_element_type=F32)
                 + lax.dot_general(qp_bf, kp, dn, preferred_element_type=F32)) * MLA_SCALE
            m_new = jnp.maximum(m, jnp.max(s, axis=-1, keepdims=True))
            alpha = jnp.exp(m - m_new)
            p = jnp.exp(s - m_new)
            l = alpha * l + jnp.sum(p, axis=-1, keepdims=True)
            acc = alpha * acc + jnp.dot(p.astype(BF16), kv, preferred_element_type=F32)
            return m_new, l, acc

        _, l, acc = lax.fori_loop(0, n_chunks, chunk_body, init)
        o_ref[b] = acc / l
        return 0

    lax.fori_loop(0, n_samples, sample_body, 0)


def _mla_sample_attention(page_table, q_lat, q_pe, ckv_new, kpe_new, cache_ckv, cache_kpe, layer,
                          ppc=SAMPLE_PAGES_PER_CHUNK):
    bsz, n_heads, rank = q_lat.shape
    assert page_table.shape[1] % (2 * ppc) == 0
    vmem = pl.BlockSpec(memory_space=pltpu.VMEM)
    return pl.pallas_call(
        functools.partial(_paged_kernel, layer=layer, ppc=ppc),
        in_specs=[pl.BlockSpec(memory_space=pltpu.SMEM), vmem, vmem, vmem, vmem,
                  pl.BlockSpec(memory_space=pl.ANY), pl.BlockSpec(memory_space=pl.ANY)],
        out_specs=vmem,
        out_shape=jax.ShapeDtypeStruct((bsz, n_heads, rank), F32),
        scratch_shapes=[pltpu.VMEM((2, ppc, PAGE_SIZE, rank), F32),
                        pltpu.VMEM((2, ppc, PAGE_SIZE, QK_ROPE_DIM), F32),
                        pltpu.SemaphoreType.DMA((2, 2))],
        compiler_params=pltpu.CompilerParams(vmem_limit_bytes=V7X_VMEM_LIMIT_BYTES),
        name="mla_sample_paged",
    )(page_table, q_lat, q_pe, ckv_new, kpe_new, cache_ckv, cache_kpe)


def _split3_bf16(x):
    x1 = x.astype(BF16)
    r1 = x - x1.astype(F32)
    x2 = r1.astype(BF16)
    x3 = (r1 - x2.astype(F32)).astype(BF16)
    return x1, x2, x3


def _dot_f32_by_01(x, mat01):
    x1, x2, x3 = _split3_bf16(x)
    out = jnp.dot(x3, mat01, preferred_element_type=F32)
    out += jnp.dot(x2, mat01, preferred_element_type=F32)
    out += jnp.dot(x1, mat01, preferred_element_type=F32)
    return out


def _dot_01_by_f32(mat01, x):
    x1, x2, x3 = _split3_bf16(x)
    out = jnp.dot(mat01, x3, preferred_element_type=F32)
    out += jnp.dot(mat01, x2, preferred_element_type=F32)
    out += jnp.dot(mat01, x1, preferred_element_type=F32)
    return out


def _ssd_kernel(xm_ref, bm_ref, cm_ref, dt_ref, dtt_ref, a_row_ref, a_col_ref, y_ref, st_ref, state):
    c = pl.program_id(1)
    q = xm_ref.shape[0]
    n_heads = dt_ref.shape[1]
    hp = xm_ref.shape[1] // n_heads
    n_state = state.shape[0]
    n_groups = bm_ref.shape[1] // n_state
    gw = xm_ref.shape[1] // n_groups
    e_heads = n_heads // n_groups

    @pl.when(c == 0)
    def _():
        state[...] = jnp.zeros_like(state)

    row = lax.broadcasted_iota(jnp.int32, (q, q), 0)
    col = lax.broadcasted_iota(jnp.int32, (q, q), 1)
    causal = col <= row
    tril = jnp.where(causal, 1.0, 0.0).astype(BF16)
    triu = jnp.where(row <= col, 1.0, 0.0).astype(BF16)
    erow = lax.broadcasted_iota(jnp.int32, (n_heads, n_heads * hp), 0)
    ecol = lax.broadcasted_iota(jnp.int32, (n_heads, n_heads * hp), 1)
    ehead = jnp.where(ecol >= erow * hp, jnp.where(ecol < (erow + 1) * hp, 1.0, 0.0), 0.0)
    expand = ehead.astype(BF16)

    dt = dt_ref[...]
    la = dt * a_row_ref[...]
    cum = _dot_01_by_f32(tril, la)
    cum_t = _dot_f32_by_01(dtt_ref[...] * a_col_ref[...], triu)
    cum_full = _dot_f32_by_01(cum, expand)
    dt_full = _dot_f32_by_01(dt, expand)
    last_full = cum_full[q - 1:q, :]
    decay_out = jnp.exp(cum_full)
    xm = xm_ref[...]
    xd = xm * dt_full
    xw = (xd * jnp.exp(last_full - cum_full)).astype(BF16)
    xd = xd.astype(BF16)
    chunk_decay = jnp.exp(last_full)

    dn_last = (((1,), (1,)), ((), ()))
    dn_first = (((0,), (0,)), ((), ()))
    for g in range(n_groups):
        bc = bm_ref[:, g * n_state:(g + 1) * n_state].astype(BF16)
        cc = cm_ref[:, g * n_state:(g + 1) * n_state].astype(BF16)
        cb = lax.dot_general(cc, bc, dn_last, preferred_element_type=F32)
        st_g = state[:, g * gw:(g + 1) * gw]
        y_off = jnp.dot(cc, st_g.astype(BF16), preferred_element_type=F32) * decay_out[:, g * gw:(g + 1) * gw]
        y_heads = []
        for e in range(e_heads):
            h = g * e_heads + e
            seg = jnp.exp(jnp.where(causal, cum[:, h:h + 1] - cum_t[h:h + 1, :], -jnp.inf))
            y_heads.append(jnp.dot((cb * seg).astype(BF16), xd[:, h * hp:(h + 1) * hp],
                                   preferred_element_type=F32))
        y_ref[:, g * gw:(g + 1) * gw] = jnp.concatenate(y_heads, axis=1) + y_off
        st_new = lax.dot_general(bc, xw[:, g * gw:(g + 1) * gw], dn_first, preferred_element_type=F32)
        state[:, g * gw:(g + 1) * gw] = st_g * chunk_decay[:, g * gw:(g + 1) * gw] + st_new

    @pl.when(c == pl.num_programs(1) - 1)
    def _():
        st_ref[...] = state[...]


def _ssd_prompt(xm, bm, cm, dt, a, bsz, seq):
    n_heads = dt.shape[1]
    hpw = xm.shape[1]
    gn = bm.shape[1]
    n_state = gn // M_GROUPS
    n_chunks = seq // CHUNK
    row_blk = lambda w: pl.BlockSpec((CHUNK, w), lambda b, c: (b * n_chunks + c, 0))
    y, st = pl.pallas_call(
        _ssd_kernel,
        grid=(bsz, n_chunks),
        in_specs=[row_blk(hpw), row_blk(gn), row_blk(gn), row_blk(n_heads),
                  pl.BlockSpec((n_heads, CHUNK), lambda b, c: (0, b * n_chunks + c)),
                  pl.BlockSpec((1, n_heads), lambda b, c: (0, 0)),
                  pl.BlockSpec((n_heads, 1), lambda b, c: (0, 0))],
        out_specs=[row_blk(hpw),
                   pl.BlockSpec((None, n_state, hpw), lambda b, c: (b, 0, 0))],
        out_shape=[jax.ShapeDtypeStruct((bsz * seq, hpw), F32),
                   jax.ShapeDtypeStruct((bsz, n_state, hpw), F32)],
        scratch_shapes=[pltpu.VMEM((n_state, hpw), F32)],
        compiler_params=_cparams(("parallel", "arbitrary")),
        name="ssd_prompt",
    )(xm, bm, cm, dt, dt.T, a.reshape(1, n_heads), a.reshape(n_heads, 1))
    return y, st


def _ret_log_gammas(n_heads):
    return [float(np.log1p(-np.exp2(np.float32(-5.0 - h), dtype=np.float32), dtype=np.float32))
            for h in range(n_heads)]


def _retention_kernel(q_ref, k_ref, v_ref, o_ref, s_ref, state):
    c = pl.program_id(1)
    qlen = q_ref.shape[0]
    n_heads = state.shape[0]
    dk = state.shape[1]
    dv = state.shape[2]

    @pl.when(c == 0)
    def _():
        state[...] = jnp.zeros_like(state)

    row = lax.broadcasted_iota(jnp.int32, (qlen, qlen), 0)
    col = lax.broadcasted_iota(jnp.int32, (qlen, qlen), 1)
    rel = (row - col).astype(F32)
    t_col = lax.broadcasted_iota(jnp.int32, (qlen, 1), 0).astype(F32)
    dn_last = (((1,), (1,)), ((), ()))
    dn_first = (((0,), (0,)), ((), ()))
    for h, lg in enumerate(_ret_log_gammas(n_heads)):
        q = q_ref[:, h * dk:(h + 1) * dk]
        k = k_ref[:, h * dk:(h + 1) * dk]
        v = v_ref[:, h * dv:(h + 1) * dv].astype(BF16)
        q_bf = q.astype(BF16)
        dmat = jnp.where(rel >= 0, jnp.exp(jnp.maximum(rel, 0.0) * lg), 0.0)
        scores = lax.dot_general(q_bf, k.astype(BF16), dn_last, preferred_element_type=F32) * dmat
        o_intra = jnp.dot(scores.astype(BF16), v, preferred_element_type=F32)
        k_w = (k * jnp.exp((qlen - 1 - t_col) * lg)).astype(BF16)
        st_new = lax.dot_general(k_w, v, dn_first, preferred_element_type=F32)
        s_prev = state[h]
        o_cross = jnp.dot(q_bf, s_prev.astype(BF16), preferred_element_type=F32) * jnp.exp((t_col + 1.0) * lg)
        o_ref[:, h * dv:(h + 1) * dv] = o_intra + o_cross
        state[h] = s_prev * float(np.exp(np.float32(qlen * lg))) + st_new

    @pl.when(c == pl.num_programs(1) - 1)
    def _():
        s_ref[...] = state[...]


def _retention_prompt(rq, rk, rv, bsz, seq):
    n_heads = rq.shape[1] // R_DK
    n_chunks = seq // CHUNK
    row_blk = lambda w: pl.BlockSpec((CHUNK, w), lambda b, c: (b * n_chunks + c, 0))
    return pl.pallas_call(
        _retention_kernel,
        grid=(bsz, n_chunks),
        in_specs=[row_blk(rq.shape[1]), row_blk(rk.shape[1]), row_blk(rv.shape[1])],
        out_specs=[row_blk(rv.shape[1]),
                   pl.BlockSpec((None, n_heads, R_DK, R_DV), lambda b, c: (b, 0, 0, 0))],
        out_shape=[jax.ShapeDtypeStruct((bsz * seq, rv.shape[1]), F32),
                   jax.ShapeDtypeStruct((bsz, n_heads, R_DK, R_DV), F32)],
        scratch_shapes=[pltpu.VMEM((n_heads, R_DK, R_DV), F32)],
        compiler_params=_cparams(("parallel", "arbitrary")),
        name="retention_prompt",
    )(rq, rk, rv)


def _rmsnorm(x, g):
    y = x * lax.rsqrt(jnp.mean(x * x, axis=-1, keepdims=True) + EPS)
    return y * g


def _gated_group_rmsnorm(y, z, g, n_groups):
    v = y * jax.nn.silu(z)
    shp = v.shape
    v = v.reshape(shp[:-1] + (n_groups, shp[-1] // n_groups))
    v = v * lax.rsqrt(jnp.mean(v * v, axis=-1, keepdims=True) + EPS)
    return v.reshape(shp) * g


def _head_groupnorm(o, g):
    mu = jnp.mean(o, axis=-1, keepdims=True)
    var = jnp.mean(jnp.square(o - mu), axis=-1, keepdims=True)
    o = (o - mu) * lax.rsqrt(var + GN_EPS)
    return o.reshape(o.shape[0], -1) * g


def _rope(x, pos):
    d = x.shape[-1]
    half = d // 2
    inv = ROPE_BASE ** (-2.0 * jnp.arange(half, dtype=F32) / d)
    ang = pos.astype(F32)[:, None] * inv[None, :]
    cos = jnp.cos(ang)[:, None, :]
    sin = jnp.sin(ang)[:, None, :]
    x1, x2 = x[..., :half], x[..., half:]
    return jnp.concatenate([x1 * cos - x2 * sin, x2 * cos + x1 * sin], axis=-1)


def _causal_dwconv_silu(full, w, b):
    L = full.shape[1] - (CONV_WIDTH - 1)
    out = b
    for t in range(CONV_WIDTH):
        out = out + full[:, t:t + L, :] * w[t]
    return jax.nn.silu(out)


def _segment(proj, idx):
    return proj[:, IN_OFFSETS_PAD[idx]:IN_OFFSETS_PAD[idx] + IN_WIDTHS[idx]]


def _mixer_block(x, pos, bsz, seq, conv_buf, ssm_h, ret_s, paged, lw):
    rows = bsz * seq
    h = _rmsnorm_rows(x, lw['norm_g'], BF16)
    proj = _mm(h, lw['w_in'])
    z, xbc, dt_raw, cq, ckv, kpe, rq, rk, rv, rg = (_segment(proj, i) for i in range(len(IN_WIDTHS)))
    full = jnp.concatenate([conv_buf, xbc.reshape(bsz, seq, CONV_DIM)], axis=1)
    new_conv = full[:, full.shape[1] - (CONV_WIDTH - 1):]
    xbc = _causal_dwconv_silu(full, lw['conv_w'], lw['conv_b']).reshape(rows, CONV_DIM)
    xm = xbc[:, :M_INNER]
    bm = xbc[:, M_INNER:M_INNER + M_GROUPS * M_STATE]
    cm = xbc[:, M_INNER + M_GROUPS * M_STATE:]
    dt = jax.nn.softplus(dt_raw + lw['dt_bias'])
    a = -jnp.exp(lw['a_log'])
    if paged is None:
        y, st = _ssd_prompt(xm, bm, cm, dt, a, bsz, seq)
        new_ssm = jnp.transpose(st.reshape(bsz, M_STATE, M_HEADS, M_HEAD_DIM), (0, 2, 3, 1))
    else:
        xm_h = xm.reshape(rows, M_HEADS, M_HEAD_DIM)
        b_h = jnp.repeat(bm.reshape(rows, M_GROUPS, M_STATE), M_HEADS // M_GROUPS, axis=1)
        c_h = jnp.repeat(cm.reshape(rows, M_GROUPS, M_STATE), M_HEADS // M_GROUPS, axis=1)
        new_ssm = (ssm_h * jnp.exp(dt * a)[:, :, None, None]
                   + (xm_h * dt[:, :, None])[:, :, :, None] * b_h[:, :, None, :])
        y = jnp.sum(new_ssm * c_h[:, :, None, :], axis=-1).reshape(rows, M_INNER)
    y = y + (lw['d_skip'][:, None] * xm.reshape(rows, M_HEADS, M_HEAD_DIM)).reshape(rows, M_INNER)
    y_m = _gated_group_rmsnorm(y, z, lw['m_norm_g'], M_GROUPS).astype(BF16)
    cq = _rmsnorm(cq, lw['q_norm_g']).astype(BF16)
    q = _mm(cq, lw['w_uq']).reshape(rows, A_HEADS, QK_NOPE_DIM + QK_ROPE_DIM)
    q_nope = q[..., :QK_NOPE_DIM]
    q_pe = _rope(q[..., QK_NOPE_DIM:], pos)
    ckv = _rmsnorm(ckv, lw['kv_norm_g'])
    kpe = _rope(kpe[:, None, :], pos)[:, 0, :]
    if paged is None:
        zpad = jnp.zeros((rows, A_HEADS, A_QK_PAD - QK_NOPE_DIM - QK_ROPE_DIM), F32)
        q_cat = jnp.concatenate([q_nope, q_pe, zpad], axis=-1).astype(BF16).reshape(rows, A_HEADS * A_QK_PAD)
        kv_in = jnp.concatenate([ckv, kpe], axis=-1).astype(BF16)
        k_cat = _mm(kv_in, lw['w_kcat'], BF16)
        v = _mm(kv_in, lw['w_vcat'], BF16)
        y_a = _mla_prompt_attention(q_cat, k_cat, v, bsz, seq).astype(BF16)
    else:
        cache_ckv, cache_kpe, page_table, layer = paged
        q_lat = jnp.einsum('bhd,rhd->bhr', q_nope, lw['w_uk'])
        o_lat = _mla_sample_attention(page_table, q_lat, q_pe, ckv, kpe, cache_ckv, cache_kpe, layer)
        y_a = jnp.einsum('bhr,rhd->bhd', o_lat, lw['w_uv']).reshape(rows, A_WIDTH).astype(BF16)
    rq = _rope(rq.reshape(rows, R_HEADS, R_DK), pos)
    rk = _rope(rk.reshape(rows, R_HEADS, R_DK), pos) * (R_DK ** -0.5)
    if paged is None:
        o, new_ret = _retention_prompt(rq.reshape(rows, R_QK_WIDTH), rk.reshape(rows, R_QK_WIDTH), rv, bsz, seq)
        o = o.reshape(rows, R_HEADS, R_DV)
    else:
        gamma = jnp.exp(jnp.log1p(-jnp.exp2(-5.0 - jnp.arange(R_HEADS, dtype=F32))))
        new_ret = (ret_s * gamma[None, :, None, None]
                   + rk[:, :, :, None] * rv.reshape(rows, R_HEADS, 1, R_DV))
        o = jnp.sum(rq[:, :, :, None] * new_ret, axis=2)
    y_r = (jax.nn.silu(rg) * _head_groupnorm(o, lw['r_norm_g'])).astype(BF16)
    gates = _mm_bias_sigmoid(h, lw['w_gate'], lw['b_gate'])
    mix = _mm_merge3(y_m, y_a, y_r, lw['w_br_m'], lw['w_br_a'], lw['w_br_r'], gates)
    x_out = _mm_residual(mix, lw['w_out'], x)
    return x_out, new_conv, new_ssm, new_ret, ckv, kpe


def _ffn_block(x, lw):
    h = _rmsnorm_rows(x, lw['norm_ffn_g'], BF16)
    a = _mm_swiglu(h, lw['w_ffn_gate'], lw['w_ffn_up'])
    return _mm_kgrid_residual(a, lw['w_ffn_down'], x, 4)


def _pad_cols(w, n):
    return jnp.pad(w, ((0, 0), (0, n - w.shape[1])))


def _layer_weights(l, norm_mix_g, w_in, conv_w, conv_b, dt_bias, a_log, d_skip, m_norm_g,
                   q_norm_g, w_uq, kv_norm_g, w_ukv, r_norm_g, w_br_m, w_br_a, w_br_r, w_gate, b_gate, w_out,
                   norm_ffn_g, w_ffn_gate, w_ffn_up, w_ffn_down):
    w_in_l = w_in[l].astype(BF16)
    segs = []
    off = 0
    for w, wp in zip(IN_WIDTHS, IN_WIDTHS_PAD):
        segs.append(_pad_cols(w_in_l[:, off:off + w], wp))
        off += w
    w3 = w_ukv[l].reshape(KV_LORA_RANK, A_HEADS, QK_NOPE_DIM + V_HEAD_DIM)
    w_uk, w_uv = w3[..., :QK_NOPE_DIM], w3[..., QK_NOPE_DIM:]
    eye = jnp.eye(QK_ROPE_DIM, dtype=F32)
    top = jnp.concatenate([w_uk, jnp.zeros((KV_LORA_RANK, A_HEADS, A_QK_PAD - QK_NOPE_DIM), F32)], axis=-1)
    bot = jnp.concatenate([jnp.zeros((QK_ROPE_DIM, A_HEADS, QK_NOPE_DIM), F32),
                           jnp.broadcast_to(eye[:, None, :], (QK_ROPE_DIM, A_HEADS, QK_ROPE_DIM)),
                           jnp.zeros((QK_ROPE_DIM, A_HEADS, A_QK_PAD - QK_NOPE_DIM - QK_ROPE_DIM), F32)], axis=-1)
    w_kcat = jnp.concatenate([top, bot], axis=0).reshape(KV_LORA_RANK + QK_ROPE_DIM, A_HEADS * A_QK_PAD)
    w_vcat = jnp.concatenate([w_uv.reshape(KV_LORA_RANK, A_WIDTH),
                              jnp.zeros((QK_ROPE_DIM, A_WIDTH), F32)], axis=0)
    return dict(
        norm_g=norm_mix_g[l], w_in=jnp.concatenate(segs, axis=1),
        conv_w=conv_w[l], conv_b=conv_b[l], dt_bias=dt_bias[l], a_log=a_log[l],
        d_skip=d_skip[l], m_norm_g=m_norm_g[l], q_norm_g=q_norm_g[l],
        w_uq=w_uq[l].astype(BF16), kv_norm_g=kv_norm_g[l],
        w_kcat=w_kcat.astype(BF16), w_vcat=w_vcat.astype(BF16), w_uk=w_uk, w_uv=w_uv,
        r_norm_g=r_norm_g[l],
        w_br_m=w_br_m[l].astype(BF16), w_br_a=w_br_a[l].astype(BF16), w_br_r=w_br_r[l].astype(BF16),
        w_gate=w_gate[l].astype(BF16), b_gate=b_gate[l], w_out=w_out[l].astype(BF16),
        norm_ffn_g=norm_ffn_g[l],
        w_ffn_gate=_pad_cols(w_ffn_gate[l].astype(BF16), D_FF_PAD),
        w_ffn_up=_pad_cols(w_ffn_up[l].astype(BF16), D_FF_PAD),
        w_ffn_down=jnp.pad(w_ffn_down[l].astype(BF16), ((0, D_FF_PAD - D_FF), (0, 0))),
    )


def kernel(x_prompt, x_sample, cache_ckv, cache_kpe, state_conv, state_ssm, state_ret, page_table,
           norm_mix_g, w_in, conv_w, conv_b, dt_bias, a_log, d_skip, m_norm_g,
           q_norm_g, w_uq, kv_norm_g, w_ukv, r_norm_g,
           w_br_m, w_br_a, w_br_r, w_gate, b_gate, w_out,
           norm_ffn_g, w_ffn_gate, w_ffn_up, w_ffn_down, norm_final_g):
    bp, lp, _ = x_prompt.shape
    bs, ls, _ = x_sample.shape
    pos_p = jnp.tile(jnp.arange(lp, dtype=jnp.int32), bp)
    pos_s = jnp.tile(PAST_LEN + jnp.arange(ls, dtype=jnp.int32), bs)
    xp = x_prompt.reshape(bp * lp, D_MODEL)
    xs = x_sample.reshape(bs * ls, D_MODEL)
    outs_p = [[] for _ in range(5)]
    outs_s = [[] for _ in range(5)]
    for l in range(DEPTH):
        lw = _layer_weights(l, norm_mix_g, w_in, conv_w, conv_b, dt_bias, a_log, d_skip, m_norm_g,
                            q_norm_g, w_uq, kv_norm_g, w_ukv, r_norm_g, w_br_m, w_br_a, w_br_r,
                            w_gate, b_gate, w_out, norm_ffn_g, w_ffn_gate, w_ffn_up, w_ffn_down)
        zc = jnp.zeros((bp, CONV_WIDTH - 1, CONV_DIM), F32)
        xp, c1, h1, r1, k1, e1 = _mixer_block(xp, pos_p, bp, lp, zc, None, None, None, lw)
        xs, c2, h2, r2, k2, e2 = _mixer_block(xs, pos_s, bs, ls, state_conv[l], state_ssm[l], state_ret[l],
                                              (cache_ckv, cache_kpe, page_table, l), lw)
        xp = _ffn_block(xp, lw)
        xs = _ffn_block(xs, lw)
        for lst, v in zip(outs_p, (k1.reshape(bp, lp, -1), e1.reshape(bp, lp, -1), c1, h1, r1)):
            lst.append(v)
        for lst, v in zip(outs_s, (k2.reshape(bs, ls, -1), e2.reshape(bs, ls, -1), c2, h2, r2)):
            lst.append(v)
    y_prompt = _rmsnorm_rows(xp, norm_final_g, F32).reshape(bp, lp, D_MODEL)
    y_sample = _rmsnorm_rows(xs, norm_final_g, F32).reshape(bs, ls, D_MODEL)
    return ((y_prompt, y_sample) + tuple(jnp.stack(v) for v in outs_p)
            + tuple(jnp.stack(v) for v in outs_s))
```

```python
import functools

import jax
import jax.numpy as jnp
import numpy as np
from jax import lax
from jax.experimental import pallas as pl
from jax.experimental.pallas import tpu as pltpu

D_MODEL = 4096
DEPTH = 2
PAST_LEN = 16384
PAGE_SIZE = 128
M_HEAD_DIM = 64
M_HEADS = D_MODEL // 128
M_INNER = M_HEADS * M_HEAD_DIM
M_GROUPS = 4
M_STATE = 128
CONV_WIDTH = 4
CONV_DIM = M_INNER + 2 * M_GROUPS * M_STATE
CHUNK = 128
A_HEADS = D_MODEL // 256
QK_NOPE_DIM = 128
QK_ROPE_DIM = 64
V_HEAD_DIM = 128
Q_LORA_RANK = 3 * D_MODEL // 16
KV_LORA_RANK = D_MODEL // 16
A_WIDTH = A_HEADS * V_HEAD_DIM
R_HEADS = D_MODEL // 512
R_DK = 128
R_DV = 256
R_QK_WIDTH = R_HEADS * R_DK
R_V_WIDTH = R_HEADS * R_DV
D_FF = -(-8 * D_MODEL // (3 * 256)) * 256
ROPE_BASE = 10000.0
EPS = 1e-6
GN_EPS = 1e-5
N_BRANCH = 3
IN_WIDTHS = (M_INNER, CONV_DIM, M_HEADS, Q_LORA_RANK, KV_LORA_RANK, QK_ROPE_DIM,
             R_QK_WIDTH, R_QK_WIDTH, R_V_WIDTH, R_V_WIDTH)

F32 = jnp.float32
BF16 = jnp.bfloat16

V7X_LANES = 128
V7X_MXU_DIM = 256
V7X_VMEM_LIMIT_BYTES = 56 * 1024 * 1024

IN_WIDTHS_PAD = tuple(-(-w // V7X_LANES) * V7X_LANES for w in IN_WIDTHS)
IN_OFFSETS_PAD = tuple(int(v) for v in np.cumsum((0,) + IN_WIDTHS_PAD[:-1]))
D_IN_PAD = sum(IN_WIDTHS_PAD)
A_QK_PAD = V7X_MXU_DIM
MLA_SCALE = (QK_NOPE_DIM + QK_ROPE_DIM) ** -0.5
FLASH_BLOCK = 512
SAMPLE_PAGES_PER_CHUNK = 16
SAMPLE_KEY_GROUPS = 2


def _cparams(semantics):
    return pltpu.CompilerParams(dimension_semantics=semantics,
                                vmem_limit_bytes=V7X_VMEM_LIMIT_BYTES)


def _pick_tile(n, prefs):
    for t in prefs:
        if n % t == 0:
            return t
    return n


def _sigmoid(x):
    return 1.0 / (1.0 + jnp.exp(-x))


def _mm_kernel(x_ref, w_ref, o_ref):
    o_ref[...] = jnp.dot(x_ref[...], w_ref[...], preferred_element_type=F32).astype(o_ref.dtype)


def _mm_bias_sigmoid_kernel(x_ref, w_ref, b_ref, o_ref):
    acc = jnp.dot(x_ref[...], w_ref[...], preferred_element_type=F32)
    o_ref[...] = _sigmoid(acc + b_ref[...]).astype(o_ref.dtype)


def _mm_residual_kernel(x_ref, w_ref, r_ref, o_ref):
    o_ref[...] = r_ref[...] + jnp.dot(x_ref[...], w_ref[...], preferred_element_type=F32)


def _mm_swiglu_kernel(x_ref, wg_ref, wu_ref, o_ref):
    x = x_ref[...]
    g = jnp.dot(x, wg_ref[...], preferred_element_type=F32)
    u = jnp.dot(x, wu_ref[...], preferred_element_type=F32)
    o_ref[...] = (g * _sigmoid(g) * u).astype(o_ref.dtype)


def _mm_merge3_kernel(ym_ref, ya_ref, yr_ref, wm_ref, wa_ref, wr_ref, gm_ref, ga_ref, gr_ref, o_ref):
    mix = gm_ref[...].astype(F32) * jnp.dot(ym_ref[...], wm_ref[...], preferred_element_type=F32)
    mix += ga_ref[...].astype(F32) * jnp.dot(ya_ref[...], wa_ref[...], preferred_element_type=F32)
    mix += gr_ref[...].astype(F32) * jnp.dot(yr_ref[...], wr_ref[...], preferred_element_type=F32)
    o_ref[...] = mix.astype(o_ref.dtype)


def _mm_kgrid_residual_kernel(x_ref, w_ref, r_ref, o_ref, acc_ref):
    k = pl.program_id(2)

    @pl.when(k == 0)
    def _():
        acc_ref[...] = r_ref[...]

    acc_ref[...] += jnp.dot(x_ref[...], w_ref[...], preferred_element_type=F32)

    @pl.when(k == pl.num_programs(2) - 1)
    def _():
        o_ref[...] = acc_ref[...]


def _row_tile(m):
    return _pick_tile(m, (1024, 1040, 512, 256, 128))


def _mm(x, w, out_dtype=F32):
    m, k = x.shape
    n = w.shape[1]
    tm = _row_tile(m)
    tn = _pick_tile(n, (1024, 896, 512, 256, 128))
    return pl.pallas_call(
        _mm_kernel,
        grid=(m // tm, n // tn),
        in_specs=[pl.BlockSpec((tm, k), lambda i, j: (i, 0)),
                  pl.BlockSpec((k, tn), lambda i, j: (0, j))],
        out_specs=pl.BlockSpec((tm, tn), lambda i, j: (i, j)),
        out_shape=jax.ShapeDtypeStruct((m, n), out_dtype),
        compiler_params=_cparams(("parallel", "parallel")),
        name="mm",
    )(x, w)


def _mm_bias_sigmoid(x, w, b):
    m, k = x.shape
    n = w.shape[1]
    tm = _row_tile(m)
    tn = _pick_tile(n, (1024, 512, 256, 128))
    return pl.pallas_call(
        _mm_bias_sigmoid_kernel,
        grid=(m // tm, n // tn),
        in_specs=[pl.BlockSpec((tm, k), lambda i, j: (i, 0)),
                  pl.BlockSpec((k, tn), lambda i, j: (0, j)),
                  pl.BlockSpec((1, tn), lambda i, j: (0, j))],
        out_specs=pl.BlockSpec((tm, tn), lambda i, j: (i, j)),
        out_shape=jax.ShapeDtypeStruct((m, n), BF16),
        compiler_params=_cparams(("parallel", "parallel")),
        name="mm_gate",
    )(x, w, b.reshape(1, n).astype(F32))


def _mm_residual(x, w, r):
    m, k = x.shape
    n = w.shape[1]
    tm = _row_tile(m)
    tn = _pick_tile(n, (512, 256, 128))
    return pl.pallas_call(
        _mm_residual_kernel,
        grid=(m // tm, n // tn),
        in_specs=[pl.BlockSpec((tm, k), lambda i, j: (i, 0)),
                  pl.BlockSpec((k, tn), lambda i, j: (0, j)),
                  pl.BlockSpec((tm, tn), lambda i, j: (i, j))],
        out_specs=pl.BlockSpec((tm, tn), lambda i, j: (i, j)),
        out_shape=jax.ShapeDtypeStruct((m, n), F32),
        compiler_params=_cparams(("parallel", "parallel")),
        name="mm_residual",
    )(x, w, r)


def _mm_swiglu(x, wg, wu):
    m, k = x.shape
    n = wg.shape[1]
    tm = _row_tile(m)
    tn = min(512, n)
    return pl.pallas_call(
        _mm_swiglu_kernel,
        grid=(m // tm, pl.cdiv(n, tn)),
        in_specs=[pl.BlockSpec((tm, k), lambda i, j: (i, 0)),
                  pl.BlockSpec((k, tn), lambda i, j: (0, j)),
                  pl.BlockSpec((k, tn), lambda i, j: (0, j))],
        out_specs=pl.BlockSpec((tm, tn), lambda i, j: (i, j)),
        out_shape=jax.ShapeDtypeStruct((m, n), BF16),
        compiler_params=_cparams(("parallel", "parallel")),
        name="mm_swiglu",
    )(x, wg, wu)


def _mm_merge3(ym, ya, yr, wm, wa, wr, gates):
    m, k = ym.shape
    n = wm.shape[1]
    tm = _row_tile(m)
    tn = _pick_tile(n, (512, 256, 128))
    nb = n // tn
    y_spec = pl.BlockSpec((tm, k), lambda i, j: (i, 0))
    w_spec = pl.BlockSpec((k, tn), lambda i, j: (0, j))
    return pl.pallas_call(
        _mm_merge3_kernel,
        grid=(m // tm, nb),
        in_specs=[y_spec, y_spec, y_spec, w_spec, w_spec, w_spec,
                  pl.BlockSpec((tm, tn), lambda i, j: (i, j)),
                  pl.BlockSpec((tm, tn), lambda i, j: (i, j + nb)),
                  pl.BlockSpec((tm, tn), lambda i, j: (i, j + 2 * nb))],
        out_specs=pl.BlockSpec((tm, tn), lambda i, j: (i, j)),
        out_shape=jax.ShapeDtypeStruct((m, n), BF16),
        compiler_params=_cparams(("parallel", "parallel")),
        name="mm_merge3",
    )(ym, ya, yr, wm, wa, wr, gates, gates, gates)


def _mm_kgrid_residual(x, w, r, n_k):
    m, k = x.shape
    n = w.shape[1]
    tm = _row_tile(m)
    tn = _pick_tile(n, (512, 256, 128))
    assert k % (n_k * V7X_LANES) == 0
    tk = k // n_k
    return pl.pallas_call(
        _mm_kgrid_residual_kernel,
        grid=(m // tm, n // tn, n_k),
        in_specs=[pl.BlockSpec((tm, tk), lambda i, j, kk: (i, kk)),
                  pl.BlockSpec((tk, tn), lambda i, j, kk: (kk, j)),
                  pl.BlockSpec((tm, tn), lambda i, j, kk: (i, j))],
        out_specs=pl.BlockSpec((tm, tn), lambda i, j, kk: (i, j)),
        out_shape=jax.ShapeDtypeStruct((m, n), F32),
        scratch_shapes=[pltpu.VMEM((tm, tn), F32)],
        compiler_params=_cparams(("parallel", "parallel", "arbitrary")),
        name="mm_kgrid_residual",
    )(x, w, r)


def _rmsnorm_kernel(x_ref, g_ref, o_ref):
    x = x_ref[...]
    y = x * lax.rsqrt(jnp.mean(x * x, axis=-1, keepdims=True) + EPS)
    o_ref[...] = (y * g_ref[...]).astype(o_ref.dtype)


def _rmsnorm_rows(x, g, out_dtype):
    m, d = x.shape
    tr = _pick_tile(m, (256, 320, 128))
    return pl.pallas_call(
        _rmsnorm_kernel,
        grid=(m // tr,),
        in_specs=[pl.BlockSpec((tr, d), lambda i: (i, 0)),
                  pl.BlockSpec((1, d), lambda i: (0, 0))],
        out_specs=pl.BlockSpec((tr, d), lambda i: (i, 0)),
        out_shape=jax.ShapeDtypeStruct((m, d), out_dtype),
        compiler_params=_cparams(("parallel",)),
        name="rmsnorm",
    )(x, g.reshape(1, d).astype(F32))


def _flash_kernel(q_ref, k_ref, v_ref, o_ref, *, blk):
    n_blocks = q_ref.shape[0] // blk
    dv = v_ref.shape[1]

    def kv_step(j, carry, q, masked):
        m, l, acc = carry
        c0 = pl.multiple_of(j * blk, blk)
        k = k_ref[pl.ds(c0, blk), :]
        v = v_ref[pl.ds(c0, blk), :]
        s = lax.dot_general(q, k, (((1,), (1,)), ((), ())), preferred_element_type=F32) * MLA_SCALE
        if masked:
            row = lax.broadcasted_iota(jnp.int32, (blk, blk), 0)
            col = lax.broadcasted_iota(jnp.int32, (blk, blk), 1)
            s = jnp.where(col <= row, s, -jnp.inf)
        m_new = jnp.maximum(m, jnp.max(s, axis=-1, keepdims=True))
        alpha = jnp.exp(m - m_new)
        p = jnp.exp(s - m_new)
        l = alpha * l + jnp.sum(p, axis=-1, keepdims=True)
        acc = alpha * acc + jnp.dot(p.astype(BF16), v, preferred_element_type=F32)
        return m_new, l, acc

    def q_block(i, _):
        r0 = pl.multiple_of(i * blk, blk)
        q = q_ref[pl.ds(r0, blk), :]
        init = (jnp.full((blk, 1), -jnp.inf, F32), jnp.zeros((blk, 1), F32), jnp.zeros((blk, dv), F32))
        carry = lax.fori_loop(0, i, lambda j, c: kv_step(j, c, q, False), init)
        _, l, acc = kv_step(i, carry, q, True)
        o_ref[pl.ds(r0, blk), :] = acc / l
        return 0

    lax.fori_loop(0, n_blocks, q_block, 0)


def _mla_prompt_attention(q_cat, k_cat, v, bsz, seq):
    n_heads = v.shape[1] // V_HEAD_DIM
    return pl.pallas_call(
        functools.partial(_flash_kernel, blk=FLASH_BLOCK),
        grid=(bsz, n_heads),
        in_specs=[pl.BlockSpec((seq, A_QK_PAD), lambda b, h: (b, h)),
                  pl.BlockSpec((seq, A_QK_PAD), lambda b, h: (b, h)),
                  pl.BlockSpec((seq, V_HEAD_DIM), lambda b, h: (b, h))],
        out_specs=pl.BlockSpec((seq, V_HEAD_DIM), lambda b, h: (b, h)),
        out_shape=jax.ShapeDtypeStruct((bsz * seq, n_heads * V_HEAD_DIM), F32),
        compiler_params=_cparams(("parallel", "parallel")),
        name="mla_prompt_flash",
    )(q_cat, k_cat, v)


def _paged_kernel(pt_ref, qlat_ref, qpe_ref, ckvn_ref, kpen_ref, ckv_hbm, kpe_hbm, o_ref,
                  ckv_buf, kpe_buf, sem, *, layer, ppc, n_split):
    n_samples, n_pages = pt_ref.shape
    n_chunks = n_pages // ppc
    pps = ppc // n_split

    def page_copies(page_of, slot):
        for p in range(ppc):
            page = page_of(p)
            yield pltpu.make_async_copy(ckv_hbm.at[layer, page], ckv_buf.at[slot, p], sem.at[0, slot])
            yield pltpu.make_async_copy(kpe_hbm.at[layer, page], kpe_buf.at[slot, p], sem.at[1, slot])

    def start_chunk(b, c, slot):
        for cp in page_copies(lambda p: pt_ref[b, c * ppc + p], slot):
            cp.start()

    def wait_chunk(slot):
        for cp in page_copies(lambda p: 0, slot):
            cp.wait()

    start_chunk(0, 0, 0)

    def sample_body(b, _):
        q = qlat_ref[b]
        qp = qpe_ref[b]
        ck_new = ckvn_ref[pl.ds(b, 1), :]
        kp_new = kpen_ref[pl.ds(b, 1), :]
        s_new = (jnp.sum(q * ck_new, axis=-1, keepdims=True)
                 + jnp.sum(qp * kp_new, axis=-1, keepdims=True)) * MLA_SCALE
        q_bf = q.astype(BF16)
        qp_bf = qp.astype(BF16)
        init = (s_new, jnp.ones_like(s_new), jnp.broadcast_to(ck_new, q.shape))

        def chunk_body(c, carry):
            m, l, acc = carry
            slot = lax.rem(c, 2)
            c1 = c + 1
            wrap = c1 == n_chunks
            nb = b + jnp.where(wrap, 1, 0)
            nc = jnp.where(wrap, 0, c1)

            @pl.when(nb < n_samples)
            def _():
                start_chunk(nb, nc, 1 - slot)

            wait_chunk(slot)
            dn = (((1,), (1,)), ((), ()))
            kvs, scores = [], []
            for j in range(n_split):
                kv = ckv_buf[slot, j * pps:(j + 1) * pps].reshape(pps * PAGE_SIZE, KV_LORA_RANK).astype(BF16)
                s_pe = jnp.concatenate(
                    [jnp.dot(qp_bf, kpe_buf[slot, j * pps + i].astype(BF16), preferred_element_type=F32)
                     for i in range(pps)], axis=1)
                s_lat = lax.dot_general(q_bf, kv, dn, preferred_element_type=F32)
                kvs.append(kv)
                scores.append((s_lat + s_pe) * MLA_SCALE)
            m_new = m
            for s in scores:
                m_new = jnp.maximum(m_new, jnp.max(s, axis=-1, keepdims=True))
            alpha = jnp.exp(m - m_new)
            l = alpha * l
            acc = alpha * acc
            for s, kv in zip(scores, kvs):
                p = jnp.exp(s - m_new)
                l = l + jnp.sum(p, axis=-1, keepdims=True)
                acc = acc + jnp.dot(p.astype(BF16), kv, preferred_element_type=F32)
            return m_new, l, acc

        _, l, acc = lax.fori_loop(0, n_chunks, chunk_body, init)
        o_ref[b] = acc / l
        return 0

    lax.fori_loop(0, n_samples, sample_body, 0)


def _mla_sample_attention(page_table, q_lat, q_pe, ckv_new, kpe_new, cache_ckv, cache_kpe_t, layer,
                          ppc=SAMPLE_PAGES_PER_CHUNK, n_split=SAMPLE_KEY_GROUPS):
    bsz, n_heads, rank = q_lat.shape
    assert page_table.shape[1] % (2 * ppc) == 0 and ppc % n_split == 0
    vmem = pl.BlockSpec(memory_space=pltpu.VMEM)
    return pl.pallas_call(
        functools.partial(_paged_kernel, layer=layer, ppc=ppc, n_split=n_split),
        in_specs=[pl.BlockSpec(memory_space=pltpu.SMEM), vmem, vmem, vmem, vmem,
                  pl.BlockSpec(memory_space=pl.ANY), pl.BlockSpec(memory_space=pl.ANY)],
        out_specs=vmem,
        out_shape=jax.ShapeDtypeStruct((bsz, n_heads, rank), F32),
        scratch_shapes=[pltpu.VMEM((2, ppc, PAGE_SIZE, rank), F32),
                        pltpu.VMEM((2, ppc, QK_ROPE_DIM, PAGE_SIZE), F32),
                        pltpu.SemaphoreType.DMA((2, 2))],
        compiler_params=pltpu.CompilerParams(vmem_limit_bytes=V7X_VMEM_LIMIT_BYTES),
        name="mla_sample_paged",
    )(page_table, q_lat, q_pe, ckv_new, kpe_new, cache_ckv, cache_kpe_t)


def _split3_bf16(x):
    x1 = x.astype(BF16)
    r1 = x - x1.astype(F32)
    x2 = r1.astype(BF16)
    x3 = (r1 - x2.astype(F32)).astype(BF16)
    return x1, x2, x3


def _dot_f32_by_01(x, mat01):
    x1, x2, x3 = _split3_bf16(x)
    out = jnp.dot(x3, mat01, preferred_element_type=F32)
    out += jnp.dot(x2, mat01, preferred_element_type=F32)
    out += jnp.dot(x1, mat01, preferred_element_type=F32)
    return out


def _dot_01_by_f32(mat01, x):
    x1, x2, x3 = _split3_bf16(x)
    out = jnp.dot(mat01, x3, preferred_element_type=F32)
    out += jnp.dot(mat01, x2, preferred_element_type=F32)
    out += jnp.dot(mat01, x1, preferred_element_type=F32)
    return out


def _ssd_kernel(xm_ref, bm_ref, cm_ref, dt_ref, dtt_ref, a_row_ref, a_col_ref, y_ref, st_ref, state):
    c = pl.program_id(1)
    q = xm_ref.shape[0]
    n_heads = dt_ref.shape[1]
    hp = xm_ref.shape[1] // n_heads
    n_state = state.shape[0]
    n_groups = bm_ref.shape[1] // n_state
    gw = xm_ref.shape[1] // n_groups
    e_heads = n_heads // n_groups

    @pl.when(c == 0)
    def _():
        state[...] = jnp.zeros_like(state)

    row = lax.broadcasted_iota(jnp.int32, (q, q), 0)
    col = lax.broadcasted_iota(jnp.int32, (q, q), 1)
    causal = col <= row
    tril = jnp.where(causal, 1.0, 0.0).astype(BF16)
    triu = jnp.where(row <= col, 1.0, 0.0).astype(BF16)
    erow = lax.broadcasted_iota(jnp.int32, (n_heads, n_heads * hp), 0)
    ecol = lax.broadcasted_iota(jnp.int32, (n_heads, n_heads * hp), 1)
    ehead = jnp.where(ecol >= erow * hp, jnp.where(ecol < (erow + 1) * hp, 1.0, 0.0), 0.0)
    expand = ehead.astype(BF16)

    dt = dt_ref[...]
    la = dt * a_row_ref[...]
    cum = _dot_01_by_f32(tril, la)
    cum_t = _dot_f32_by_01(dtt_ref[...] * a_col_ref[...], triu)
    cum_full = _dot_f32_by_01(cum, expand)
    dt_full = _dot_f32_by_01(dt, expand)
    last_full = cum_full[q - 1:q, :]
    decay_out = jnp.exp(cum_full)
    xm = xm_ref[...]
    xd = xm * dt_full
    xw = (xd * jnp.exp(last_full - cum_full)).astype(BF16)
    xd = xd.astype(BF16)
    chunk_decay = jnp.exp(last_full)

    dn_last = (((1,), (1,)), ((), ()))
    dn_first = (((0,), (0,)), ((), ()))
    for g in range(n_groups):
        bc = bm_ref[:, g * n_state:(g + 1) * n_state].astype(BF16)
        cc = cm_ref[:, g * n_state:(g + 1) * n_state].astype(BF16)
        cb = lax.dot_general(cc, bc, dn_last, preferred_element_type=F32)
        st_g = state[:, g * gw:(g + 1) * gw]
        y_off = jnp.dot(cc, st_g.astype(BF16), preferred_element_type=F32) * decay_out[:, g * gw:(g + 1) * gw]
        y_heads = []
        for e in range(e_heads):
            h = g * e_heads + e
            seg = jnp.exp(jnp.where(causal, cum[:, h:h + 1] - cum_t[h:h + 1, :], -jnp.inf))
            y_heads.append(jnp.dot((cb * seg).astype(BF16), xd[:, h * hp:(h + 1) * hp],
                                   preferred_element_type=F32))
        y_ref[:, g * gw:(g + 1) * gw] = jnp.concatenate(y_heads, axis=1) + y_off
        st_new = lax.dot_general(bc, xw[:, g * gw:(g + 1) * gw], dn_first, preferred_element_type=F32)
        state[:, g * gw:(g + 1) * gw] = st_g * chunk_decay[:, g * gw:(g + 1) * gw] + st_new

    @pl.when(c == pl.num_programs(1) - 1)
    def _():
        st_ref[...] = state[...]


def _ssd_prompt(xm, bm, cm, dt, a, bsz, seq):
    n_heads = dt.shape[1]
    hpw = xm.shape[1]
    gn = bm.shape[1]
    n_state = gn // M_GROUPS
    n_chunks = seq // CHUNK
    row_blk = lambda w: pl.BlockSpec((CHUNK, w), lambda b, c: (b * n_chunks + c, 0))
    y, st = pl.pallas_call(
        _ssd_kernel,
        grid=(bsz, n_chunks),
        in_specs=[row_blk(hpw), row_blk(gn), row_blk(gn), row_blk(n_heads),
                  pl.BlockSpec((n_heads, CHUNK), lambda b, c: (0, b * n_chunks + c)),
                  pl.BlockSpec((1, n_heads), lambda b, c: (0, 0)),
                  pl.BlockSpec((n_heads, 1), lambda b, c: (0, 0))],
        out_specs=[row_blk(hpw),
                   pl.BlockSpec((None, n_state, hpw), lambda b, c: (b, 0, 0))],
        out_shape=[jax.ShapeDtypeStruct((bsz * seq, hpw), F32),
                   jax.ShapeDtypeStruct((bsz, n_state, hpw), F32)],
        scratch_shapes=[pltpu.VMEM((n_state, hpw), F32)],
        compiler_params=_cparams(("parallel", "arbitrary")),
        name="ssd_prompt",
    )(xm, bm, cm, dt, dt.T, a.reshape(1, n_heads), a.reshape(n_heads, 1))
    return y, st


def _ret_log_gammas(n_heads):
    return [float(np.log1p(-np.exp2(np.float32(-5.0 - h), dtype=np.float32), dtype=np.float32))
            for h in range(n_heads)]


def _retention_kernel(q_ref, k_ref, v_ref, o_ref, s_ref, state):
    c = pl.program_id(1)
    qlen = q_ref.shape[0]
    n_heads = state.shape[0]
    dk = state.shape[1]
    dv = state.shape[2]

    @pl.when(c == 0)
    def _():
        state[...] = jnp.zeros_like(state)

    row = lax.broadcasted_iota(jnp.int32, (qlen, qlen), 0)
    col = lax.broadcasted_iota(jnp.int32, (qlen, qlen), 1)
    rel = (row - col).astype(F32)
    t_col = lax.broadcasted_iota(jnp.int32, (qlen, 1), 0).astype(F32)
    dn_last = (((1,), (1,)), ((), ()))
    dn_first = (((0,), (0,)), ((), ()))
    for h, lg in enumerate(_ret_log_gammas(n_heads)):
        q = q_ref[:, h * dk:(h + 1) * dk]
        k = k_ref[:, h * dk:(h + 1) * dk]
        v = v_ref[:, h * dv:(h + 1) * dv].astype(BF16)
        q_bf = q.astype(BF16)
        dmat = jnp.where(rel >= 0, jnp.exp(jnp.maximum(rel, 0.0) * lg), 0.0)
        scores = lax.dot_general(q_bf, k.astype(BF16), dn_last, preferred_element_type=F32) * dmat
        o_intra = jnp.dot(scores.astype(BF16), v, preferred_element_type=F32)
        k_w = (k * jnp.exp((qlen - 1 - t_col) * lg)).astype(BF16)
        st_new = lax.dot_general(k_w, v, dn_first, preferred_element_type=F32)
        s_prev = state[h]
        o_cross = jnp.dot(q_bf, s_prev.astype(BF16), preferred_element_type=F32) * jnp.exp((t_col + 1.0) * lg)
        o_ref[:, h * dv:(h + 1) * dv] = o_intra + o_cross
        state[h] = s_prev * float(np.exp(np.float32(qlen * lg))) + st_new

    @pl.when(c == pl.num_programs(1) - 1)
    def _():
        s_ref[...] = state[...]


def _retention_prompt(rq, rk, rv, bsz, seq):
    n_heads = rq.shape[1] // R_DK
    n_chunks = seq // CHUNK
    row_blk = lambda w: pl.BlockSpec((CHUNK, w), lambda b, c: (b * n_chunks + c, 0))
    return pl.pallas_call(
        _retention_kernel,
        grid=(bsz, n_chunks),
        in_specs=[row_blk(rq.shape[1]), row_blk(rk.shape[1]), row_blk(rv.shape[1])],
        out_specs=[row_blk(rv.shape[1]),
                   pl.BlockSpec((None, n_heads, R_DK, R_DV), lambda b, c: (b, 0, 0, 0))],
        out_shape=[jax.ShapeDtypeStruct((bsz * seq, rv.shape[1]), F32),
                   jax.ShapeDtypeStruct((bsz, n_heads, R_DK, R_DV), F32)],
        scratch_shapes=[pltpu.VMEM((n_heads, R_DK, R_DV), F32)],
        compiler_params=_cparams(("parallel", "arbitrary")),
        name="retention_prompt",
    )(rq, rk, rv)


def _rmsnorm(x, g):
    y = x * lax.rsqrt(jnp.mean(x * x, axis=-1, keepdims=True) + EPS)
    return y * g


def _gated_group_rmsnorm(y, z, g, n_groups):
    v = y * jax.nn.silu(z)
    shp = v.shape
    v = v.reshape(shp[:-1] + (n_groups, shp[-1] // n_groups))
    v = v * lax.rsqrt(jnp.mean(v * v, axis=-1, keepdims=True) + EPS)
    return v.reshape(shp) * g


def _head_groupnorm(o, g):
    mu = jnp.mean(o, axis=-1, keepdims=True)
    var = jnp.mean(jnp.square(o - mu), axis=-1, keepdims=True)
    o = (o - mu) * lax.rsqrt(var + GN_EPS)
    return o.reshape(o.shape[0], -1) * g


def _rope(x, pos):
    d = x.shape[-1]
    half = d // 2
    inv = ROPE_BASE ** (-2.0 * jnp.arange(half, dtype=F32) / d)
    ang = pos.astype(F32)[:, None] * inv[None, :]
    cos = jnp.cos(ang)[:, None, :]
    sin = jnp.sin(ang)[:, None, :]
    x1, x2 = x[..., :half], x[..., half:]
    return jnp.concatenate([x1 * cos - x2 * sin, x2 * cos + x1 * sin], axis=-1)


def _causal_dwconv_silu(full, w, b):
    L = full.shape[1] - (CONV_WIDTH - 1)
    out = b
    for t in range(CONV_WIDTH):
        out = out + full[:, t:t + L, :] * w[t]
    return jax.nn.silu(out)


def _segment(proj, idx):
    return proj[:, IN_OFFSETS_PAD[idx]:IN_OFFSETS_PAD[idx] + IN_WIDTHS[idx]]


def _branches(proj, q, pos, bsz, seq, conv_buf, ssm_h, ret_s, paged, lw):
    rows = bsz * seq
    z, xbc, dt_raw, _, ckv, kpe, rq, rk, rv, rg = (_segment(proj, i) for i in range(len(IN_WIDTHS)))
    full = jnp.concatenate([conv_buf, xbc.reshape(bsz, seq, CONV_DIM)], axis=1)
    new_conv = full[:, full.shape[1] - (CONV_WIDTH - 1):]
    xbc = _causal_dwconv_silu(full, lw['conv_w'], lw['conv_b']).reshape(rows, CONV_DIM)
    xm = xbc[:, :M_INNER]
    bm = xbc[:, M_INNER:M_INNER + M_GROUPS * M_STATE]
    cm = xbc[:, M_INNER + M_GROUPS * M_STATE:]
    dt = jax.nn.softplus(dt_raw + lw['dt_bias'])
    a = -jnp.exp(lw['a_log'])
    if paged is None:
        y, st = _ssd_prompt(xm, bm, cm, dt, a, bsz, seq)
        new_ssm = jnp.transpose(st.reshape(bsz, M_STATE, M_HEADS, M_HEAD_DIM), (0, 2, 3, 1))
    else:
        xdt = xm.reshape(rows, M_HEADS, M_HEAD_DIM) * dt[:, :, None]
        b_h = jnp.repeat(bm.reshape(rows, M_GROUPS, M_STATE), M_HEADS // M_GROUPS, axis=1)
        c_h = jnp.repeat(cm.reshape(rows, M_GROUPS, M_STATE), M_HEADS // M_GROUPS, axis=1)
        decay = jnp.exp(dt * a)
        y = (decay[:, :, None] * jnp.sum(ssm_h * c_h[:, :, None, :], axis=-1)
             + xdt * jnp.sum(b_h * c_h, axis=-1)[:, :, None]).reshape(rows, M_INNER)
        new_ssm = (decay, xdt, b_h)
    y = y + (lw['d_skip'][:, None] * xm.reshape(rows, M_HEADS, M_HEAD_DIM)).reshape(rows, M_INNER)
    y_m = _gated_group_rmsnorm(y, z, lw['m_norm_g'], M_GROUPS).astype(BF16)
    q = q.reshape(rows, A_HEADS, QK_NOPE_DIM + QK_ROPE_DIM)
    q_nope = q[..., :QK_NOPE_DIM]
    q_pe = _rope(q[..., QK_NOPE_DIM:], pos)
    ckv = _rmsnorm(ckv, lw['kv_norm_g'])
    kpe = _rope(kpe[:, None, :], pos)[:, 0, :]
    if paged is None:
        zpad = jnp.zeros((rows, A_HEADS, A_QK_PAD - QK_NOPE_DIM - QK_ROPE_DIM), F32)
        q_cat = jnp.concatenate([q_nope, q_pe, zpad], axis=-1).astype(BF16).reshape(rows, A_HEADS * A_QK_PAD)
        kv_in = jnp.concatenate([ckv, kpe], axis=-1).astype(BF16)
        k_cat = _mm(kv_in, lw['w_kcat'], BF16)
        v = _mm(kv_in, lw['w_vcat'], BF16)
        y_a = _mla_prompt_attention(q_cat, k_cat, v, bsz, seq).astype(BF16)
    else:
        cache_ckv, cache_kpe_t, page_table, layer = paged
        q_lat = jnp.einsum('bhd,rhd->bhr', q_nope, lw['w_uk'])
        o_lat = _mla_sample_attention(page_table, q_lat, q_pe, ckv, kpe, cache_ckv, cache_kpe_t, layer)
        y_a = jnp.einsum('bhr,rhd->bhd', o_lat, lw['w_uv']).reshape(rows, A_WIDTH).astype(BF16)
    rq = _rope(rq.reshape(rows, R_HEADS, R_DK), pos)
    rk = _rope(rk.reshape(rows, R_HEADS, R_DK), pos) * (R_DK ** -0.5)
    if paged is None:
        o, new_ret = _retention_prompt(rq.reshape(rows, R_QK_WIDTH), rk.reshape(rows, R_QK_WIDTH), rv, bsz, seq)
        o = o.reshape(rows, R_HEADS, R_DV)
    else:
        gamma = _ret_gamma()
        rv_h = rv.reshape(rows, R_HEADS, R_DV)
        o = (gamma[None, :, None] * jnp.sum(rq[:, :, :, None] * ret_s, axis=2)
             + jnp.sum(rq * rk, axis=-1)[:, :, None] * rv_h)
        new_ret = (rk, rv_h)
    y_r = (jax.nn.silu(rg) * _head_groupnorm(o, lw['r_norm_g'])).astype(BF16)
    return (y_m, y_a, y_r), new_conv, new_ssm, new_ret, ckv, kpe


def _mixer_block(x, n_prompt, prompt_args, sample_args, lw):
    h = _rmsnorm_rows(x, lw['norm_g'], BF16)
    proj = _mm(h, lw['w_in'])
    cq = _rmsnorm(_segment(proj, 3), lw['q_norm_g']).astype(BF16)
    q = _mm(cq, lw['w_uq'])
    ys_p, *aux_p = _branches(proj[:n_prompt], q[:n_prompt], *prompt_args, lw)
    ys_s, *aux_s = _branches(proj[n_prompt:], q[n_prompt:], *sample_args, lw)
    y_m, y_a, y_r = (jnp.concatenate([yp, ys], axis=0) for yp, ys in zip(ys_p, ys_s))
    gates = _mm_bias_sigmoid(h, lw['w_gate'], lw['b_gate'])
    mix = _mm_merge3(y_m, y_a, y_r, lw['w_br_m'], lw['w_br_a'], lw['w_br_r'], gates)
    x_out = _mm_residual(mix, lw['w_out'], x)
    return x_out, aux_p, aux_s


def _ret_gamma():
    return jnp.exp(jnp.log1p(-jnp.exp2(-5.0 - jnp.arange(R_HEADS, dtype=F32))))


def _sample_state_updates(state_ssm, state_ret, ssm_upd, ret_upd):
    decay, xdt, b_h = (jnp.stack(v) for v in zip(*ssm_upd))
    s_ssm = state_ssm * decay[..., None, None] + xdt[..., None] * b_h[..., None, :]
    rk, rv_h = (jnp.stack(v) for v in zip(*ret_upd))
    s_ret = state_ret * _ret_gamma()[None, None, :, None, None] + rk[..., None] * rv_h[..., None, :]
    return s_ssm, s_ret


def _ffn_block(x, lw):
    h = _rmsnorm_rows(x, lw['norm_ffn_g'], BF16)
    a = _mm_swiglu(h, lw['w_ffn_gate'], lw['w_ffn_up'])
    return _mm_kgrid_residual(a, lw['w_ffn_down'], x, 2)


def _pad_cols(w, n):
    return jnp.pad(w, ((0, 0), (0, n - w.shape[1])))


def _layer_weights(l, norm_mix_g, w_in, conv_w, conv_b, dt_bias, a_log, d_skip, m_norm_g,
                   q_norm_g, w_uq, kv_norm_g, w_ukv, r_norm_g, w_br_m, w_br_a, w_br_r, w_gate, b_gate, w_out,
                   norm_ffn_g, w_ffn_gate, w_ffn_up, w_ffn_down):
    w_in_l = w_in[l].astype(BF16)
    segs = []
    off = 0
    for w, wp in zip(IN_WIDTHS, IN_WIDTHS_PAD):
        segs.append(_pad_cols(w_in_l[:, off:off + w], wp))
        off += w
    w3 = w_ukv[l].reshape(KV_LORA_RANK, A_HEADS, QK_NOPE_DIM + V_HEAD_DIM)
    w_uk, w_uv = w3[..., :QK_NOPE_DIM], w3[..., QK_NOPE_DIM:]
    eye = jnp.eye(QK_ROPE_DIM, dtype=F32)
    top = jnp.concatenate([w_uk, jnp.zeros((KV_LORA_RANK, A_HEADS, A_QK_PAD - QK_NOPE_DIM), F32)], axis=-1)
    bot = jnp.concatenate([jnp.zeros((QK_ROPE_DIM, A_HEADS, QK_NOPE_DIM), F32),
                           jnp.broadcast_to(eye[:, None, :], (QK_ROPE_DIM, A_HEADS, QK_ROPE_DIM)),
                           jnp.zeros((QK_ROPE_DIM, A_HEADS, A_QK_PAD - QK_NOPE_DIM - QK_ROPE_DIM), F32)], axis=-1)
    w_kcat = jnp.concatenate([top, bot], axis=0).reshape(KV_LORA_RANK + QK_ROPE_DIM, A_HEADS * A_QK_PAD)
    w_vcat = jnp.concatenate([w_uv.reshape(KV_LORA_RANK, A_WIDTH),
                              jnp.zeros((QK_ROPE_DIM, A_WIDTH), F32)], axis=0)
    return dict(
        norm_g=norm_mix_g[l], w_in=jnp.concatenate(segs, axis=1),
        conv_w=conv_w[l], conv_b=conv_b[l], dt_bias=dt_bias[l], a_log=a_log[l],
        d_skip=d_skip[l], m_norm_g=m_norm_g[l], q_norm_g=q_norm_g[l],
        w_uq=w_uq[l].astype(BF16), kv_norm_g=kv_norm_g[l],
        w_kcat=w_kcat.astype(BF16), w_vcat=w_vcat.astype(BF16), w_uk=w_uk, w_uv=w_uv,
        r_norm_g=r_norm_g[l],
        w_br_m=w_br_m[l].astype(BF16), w_br_a=w_br_a[l].astype(BF16), w_br_r=w_br_r[l].astype(BF16),
        w_gate=w_gate[l].astype(BF16), b_gate=b_gate[l], w_out=w_out[l].astype(BF16),
        norm_ffn_g=norm_ffn_g[l],
        w_ffn_gate=w_ffn_gate[l].astype(BF16), w_ffn_up=w_ffn_up[l].astype(BF16),
        w_ffn_down=w_ffn_down[l].astype(BF16),
    )


def kernel(x_prompt, x_sample, cache_ckv, cache_kpe, state_conv, state_ssm, state_ret, page_table,
           norm_mix_g, w_in, conv_w, conv_b, dt_bias, a_log, d_skip, m_norm_g,
           q_norm_g, w_uq, kv_norm_g, w_ukv, r_norm_g,
           w_br_m, w_br_a, w_br_r, w_gate, b_gate, w_out,
           norm_ffn_g, w_ffn_gate, w_ffn_up, w_ffn_down, norm_final_g):
    bp, lp, _ = x_prompt.shape
    bs, ls, _ = x_sample.shape
    pos_p = jnp.tile(jnp.arange(lp, dtype=jnp.int32), bp)
    pos_s = jnp.tile(PAST_LEN + jnp.arange(ls, dtype=jnp.int32), bs)
    n_prompt = bp * lp
    x = jnp.concatenate([x_prompt.reshape(n_prompt, D_MODEL), x_sample.reshape(bs * ls, D_MODEL)], axis=0)
    cache_kpe_t = jnp.swapaxes(cache_kpe, 2, 3)
    outs_p = [[] for _ in range(5)]
    outs_s = [[] for _ in range(5)]
    for l in range(DEPTH):
        lw = _layer_weights(l, norm_mix_g, w_in, conv_w, conv_b, dt_bias, a_log, d_skip, m_norm_g,
                            q_norm_g, w_uq, kv_norm_g, w_ukv, r_norm_g, w_br_m, w_br_a, w_br_r,
                            w_gate, b_gate, w_out, norm_ffn_g, w_ffn_gate, w_ffn_up, w_ffn_down)
        zc = jnp.zeros((bp, CONV_WIDTH - 1, CONV_DIM), F32)
        x, (c1, h1, r1, k1, e1), (c2, h2, r2, k2, e2) = _mixer_block(
            x, n_prompt, (pos_p, bp, lp, zc, None, None, None),
            (pos_s, bs, ls, state_conv[l], state_ssm[l], state_ret[l], (cache_ckv, cache_kpe_t, page_table, l)), lw)
        x = _ffn_block(x, lw)
        for lst, v in zip(outs_p, (k1.reshape(bp, lp, -1), e1.reshape(bp, lp, -1), c1, h1, r1)):
            lst.append(v)
        for lst, v in zip(outs_s, (k2.reshape(bs, ls, -1), e2.reshape(bs, ls, -1), c2, h2, r2)):
            lst.append(v)
    y = _rmsnorm_rows(x, norm_final_g, F32)
    y_prompt = y[:n_prompt].reshape(bp, lp, D_MODEL)
    y_sample = y[n_prompt:].reshape(bs, ls, D_MODEL)
    s_ssm, s_ret = _sample_state_updates(state_ssm, state_ret, outs_s[3], outs_s[4])
    return ((y_prompt, y_sample) + tuple(jnp.stack(v) for v in outs_p)
            + tuple(jnp.stack(v) for v in outs_s[:3]) + (s_ssm, s_ret))
```

```python
import functools

import jax
import jax.numpy as jnp
import numpy as np
from jax import lax
from jax.experimental import pallas as pl
from jax.experimental.pallas import tpu as pltpu

D_MODEL = 4096
DEPTH = 2
PAST_LEN = 16384
PAGE_SIZE = 128
M_HEAD_DIM = 64
M_HEADS = D_MODEL // 128
M_INNER = M_HEADS * M_HEAD_DIM
M_GROUPS = 4
M_STATE = 128
CONV_WIDTH = 4
CONV_DIM = M_INNER + 2 * M_GROUPS * M_STATE
CHUNK = 128
A_HEADS = D_MODEL // 256
QK_NOPE_DIM = 128
QK_ROPE_DIM = 64
V_HEAD_DIM = 128
Q_LORA_RANK = 3 * D_MODEL // 16
KV_LORA_RANK = D_MODEL // 16
A_WIDTH = A_HEADS * V_HEAD_DIM
R_HEADS = D_MODEL // 512
R_DK = 128
R_DV = 256
R_QK_WIDTH = R_HEADS * R_DK
R_V_WIDTH = R_HEADS * R_DV
D_FF = -(-8 * D_MODEL // (3 * 256)) * 256
ROPE_BASE = 10000.0
EPS = 1e-6
GN_EPS = 1e-5
N_BRANCH = 3
IN_WIDTHS = (M_INNER, CONV_DIM, M_HEADS, Q_LORA_RANK, KV_LORA_RANK, QK_ROPE_DIM,
             R_QK_WIDTH, R_QK_WIDTH, R_V_WIDTH, R_V_WIDTH)

F32 = jnp.float32
BF16 = jnp.bfloat16

V7X_LANES = 128
V7X_MXU_DIM = 256
V7X_VMEM_LIMIT_BYTES = 56 * 1024 * 1024

IN_WIDTHS_PAD = tuple(-(-w // V7X_LANES) * V7X_LANES for w in IN_WIDTHS)
IN_OFFSETS_PAD = tuple(int(v) for v in np.cumsum((0,) + IN_WIDTHS_PAD[:-1]))
D_IN_PAD = sum(IN_WIDTHS_PAD)
A_QK_PAD = V7X_MXU_DIM
MLA_SCALE = (QK_NOPE_DIM + QK_ROPE_DIM) ** -0.5
FLASH_BLOCK = 512
SAMPLE_PAGES_PER_CHUNK = 16
SAMPLE_KEY_GROUPS = 2


def _cparams(semantics):
    return pltpu.CompilerParams(dimension_semantics=semantics,
                                vmem_limit_bytes=V7X_VMEM_LIMIT_BYTES)


def _pick_tile(n, prefs):
    for t in prefs:
        if n % t == 0:
            return t
    return n


def _sigmoid(x):
    return 1.0 / (1.0 + jnp.exp(-x))


def _mm_kernel(x_ref, w_ref, o_ref):
    o_ref[...] = jnp.dot(x_ref[...], w_ref[...], preferred_element_type=F32).astype(o_ref.dtype)


def _mm_bias_sigmoid_kernel(x_ref, w_ref, b_ref, o_ref):
    acc = jnp.dot(x_ref[...], w_ref[...], preferred_element_type=F32)
    o_ref[...] = _sigmoid(acc + b_ref[...]).astype(o_ref.dtype)


def _mm_residual_kernel(x_ref, w_ref, r_ref, o_ref):
    o_ref[...] = r_ref[...] + jnp.dot(x_ref[...], w_ref[...], preferred_element_type=F32)


def _mm_swiglu_kernel(x_ref, wg_ref, wu_ref, o_ref):
    x = x_ref[...]
    g = jnp.dot(x, wg_ref[...], preferred_element_type=F32)
    u = jnp.dot(x, wu_ref[...], preferred_element_type=F32)
    o_ref[...] = (g * _sigmoid(g) * u).astype(o_ref.dtype)


def _mm_merge3_kernel(ym_ref, ya_ref, yr_ref, wm_ref, wa_ref, wr_ref, gm_ref, ga_ref, gr_ref, o_ref):
    mix = gm_ref[...].astype(F32) * jnp.dot(ym_ref[...], wm_ref[...], preferred_element_type=F32)
    mix += ga_ref[...].astype(F32) * jnp.dot(ya_ref[...], wa_ref[...], preferred_element_type=F32)
    mix += gr_ref[...].astype(F32) * jnp.dot(yr_ref[...], wr_ref[...], preferred_element_type=F32)
    o_ref[...] = mix.astype(o_ref.dtype)


def _mm_kgrid_residual_kernel(x_ref, w_ref, r_ref, o_ref, acc_ref):
    k = pl.program_id(2)

    @pl.when(k == 0)
    def _():
        acc_ref[...] = r_ref[...]

    acc_ref[...] += jnp.dot(x_ref[...], w_ref[...], preferred_element_type=F32)

    @pl.when(k == pl.num_programs(2) - 1)
    def _():
        o_ref[...] = acc_ref[...]


def _row_tile(m):
    return _pick_tile(m, (1024, 1040, 512, 256, 128))


def _weight_operand(w, rows, cols, block_index):
    if isinstance(w, tuple):
        arr, layer = w
        return arr, pl.BlockSpec((None, rows, cols), lambda *g: (layer,) + block_index(*g))
    return w, pl.BlockSpec((rows, cols), block_index)


def _weight_shape(w):
    return (w[0] if isinstance(w, tuple) else w).shape[-2:]


def _mm(x, w, out_dtype=F32):
    m, k = x.shape
    n = _weight_shape(w)[1]
    tm = _row_tile(m)
    tn = _pick_tile(n, (1024, 896, 512, 256, 128))
    w, w_spec = _weight_operand(w, k, tn, lambda i, j: (0, j))
    return pl.pallas_call(
        _mm_kernel,
        grid=(m // tm, n // tn),
        in_specs=[pl.BlockSpec((tm, k), lambda i, j: (i, 0)), w_spec],
        out_specs=pl.BlockSpec((tm, tn), lambda i, j: (i, j)),
        out_shape=jax.ShapeDtypeStruct((m, n), out_dtype),
        compiler_params=_cparams(("parallel", "parallel")),
        name="mm",
    )(x, w)


def _mm_bias_sigmoid(x, w, b):
    m, k = x.shape
    n = _weight_shape(w)[1]
    tm = _row_tile(m)
    tn = _pick_tile(n, (1024, 512, 256, 128))
    w, w_spec = _weight_operand(w, k, tn, lambda i, j: (0, j))
    return pl.pallas_call(
        _mm_bias_sigmoid_kernel,
        grid=(m // tm, n // tn),
        in_specs=[pl.BlockSpec((tm, k), lambda i, j: (i, 0)), w_spec,
                  pl.BlockSpec((1, tn), lambda i, j: (0, j))],
        out_specs=pl.BlockSpec((tm, tn), lambda i, j: (i, j)),
        out_shape=jax.ShapeDtypeStruct((m, n), BF16),
        compiler_params=_cparams(("parallel", "parallel")),
        name="mm_gate",
    )(x, w, b.reshape(1, n).astype(F32))


def _mm_residual(x, w, r):
    m, k = x.shape
    n = _weight_shape(w)[1]
    tm = _row_tile(m)
    tn = _pick_tile(n, (512, 256, 128))
    w, w_spec = _weight_operand(w, k, tn, lambda i, j: (0, j))
    return pl.pallas_call(
        _mm_residual_kernel,
        grid=(m // tm, n // tn),
        in_specs=[pl.BlockSpec((tm, k), lambda i, j: (i, 0)), w_spec,
                  pl.BlockSpec((tm, tn), lambda i, j: (i, j))],
        out_specs=pl.BlockSpec((tm, tn), lambda i, j: (i, j)),
        out_shape=jax.ShapeDtypeStruct((m, n), F32),
        compiler_params=_cparams(("parallel", "parallel")),
        name="mm_residual",
    )(x, w, r)


def _mm_swiglu(x, wg, wu):
    m, k = x.shape
    n = _weight_shape(wg)[1]
    tm = _row_tile(m)
    tn = min(512, n)
    wg, wg_spec = _weight_operand(wg, k, tn, lambda i, j: (0, j))
    wu, wu_spec = _weight_operand(wu, k, tn, lambda i, j: (0, j))
    return pl.pallas_call(
        _mm_swiglu_kernel,
        grid=(m // tm, pl.cdiv(n, tn)),
        in_specs=[pl.BlockSpec((tm, k), lambda i, j: (i, 0)), wg_spec, wu_spec],
        out_specs=pl.BlockSpec((tm, tn), lambda i, j: (i, j)),
        out_shape=jax.ShapeDtypeStruct((m, n), BF16),
        compiler_params=_cparams(("parallel", "parallel")),
        name="mm_swiglu",
    )(x, wg, wu)


def _mm_merge3(ym, ya, yr, wm, wa, wr, gates):
    m, k = ym.shape
    n = _weight_shape(wm)[1]
    tm = _row_tile(m)
    tn = _pick_tile(n, (512, 256, 128))
    nb = n // tn
    y_spec = pl.BlockSpec((tm, k), lambda i, j: (i, 0))
    (wm, wm_spec), (wa, wa_spec), (wr, wr_spec) = (
        _weight_operand(w, k, tn, lambda i, j: (0, j)) for w in (wm, wa, wr))
    return pl.pallas_call(
        _mm_merge3_kernel,
        grid=(m // tm, nb),
        in_specs=[y_spec, y_spec, y_spec, wm_spec, wa_spec, wr_spec,
                  pl.BlockSpec((tm, tn), lambda i, j: (i, j)),
                  pl.BlockSpec((tm, tn), lambda i, j: (i, j + nb)),
                  pl.BlockSpec((tm, tn), lambda i, j: (i, j + 2 * nb))],
        out_specs=pl.BlockSpec((tm, tn), lambda i, j: (i, j)),
        out_shape=jax.ShapeDtypeStruct((m, n), BF16),
        compiler_params=_cparams(("parallel", "parallel")),
        name="mm_merge3",
    )(ym, ya, yr, wm, wa, wr, gates, gates, gates)


def _mm_kgrid_residual(x, w, r, n_k):
    m, k = x.shape
    n = _weight_shape(w)[1]
    tm = _row_tile(m)
    tn = _pick_tile(n, (512, 256, 128))
    assert k % (n_k * V7X_LANES) == 0
    tk = k // n_k
    w, w_spec = _weight_operand(w, tk, tn, lambda i, j, kk: (kk, j))
    return pl.pallas_call(
        _mm_kgrid_residual_kernel,
        grid=(m // tm, n // tn, n_k),
        in_specs=[pl.BlockSpec((tm, tk), lambda i, j, kk: (i, kk)), w_spec,
                  pl.BlockSpec((tm, tn), lambda i, j, kk: (i, j))],
        out_specs=pl.BlockSpec((tm, tn), lambda i, j, kk: (i, j)),
        out_shape=jax.ShapeDtypeStruct((m, n), F32),
        scratch_shapes=[pltpu.VMEM((tm, tn), F32)],
        compiler_params=_cparams(("parallel", "parallel", "arbitrary")),
        name="mm_kgrid_residual",
    )(x, w, r)


def _rmsnorm_kernel(x_ref, g_ref, o_ref):
    x = x_ref[...]
    y = x * lax.rsqrt(jnp.mean(x * x, axis=-1, keepdims=True) + EPS)
    o_ref[...] = (y * g_ref[...]).astype(o_ref.dtype)


def _rmsnorm_rows(x, g, out_dtype):
    m, d = x.shape
    tr = _pick_tile(m, (256, 320, 128))
    return pl.pallas_call(
        _rmsnorm_kernel,
        grid=(m // tr,),
        in_specs=[pl.BlockSpec((tr, d), lambda i: (i, 0)),
                  pl.BlockSpec((1, d), lambda i: (0, 0))],
        out_specs=pl.BlockSpec((tr, d), lambda i: (i, 0)),
        out_shape=jax.ShapeDtypeStruct((m, d), out_dtype),
        compiler_params=_cparams(("parallel",)),
        name="rmsnorm",
    )(x, g.reshape(1, d).astype(F32))


def _flash_kernel(q_ref, k_ref, v_ref, o_ref, *, blk):
    n_blocks = q_ref.shape[0] // blk
    dv = v_ref.shape[1]

    def kv_step(j, carry, q, masked):
        m, l, acc = carry
        c0 = pl.multiple_of(j * blk, blk)
        k = k_ref[pl.ds(c0, blk), :]
        v = v_ref[pl.ds(c0, blk), :]
        s = lax.dot_general(q, k, (((1,), (1,)), ((), ())), preferred_element_type=F32) * MLA_SCALE
        if masked:
            row = lax.broadcasted_iota(jnp.int32, (blk, blk), 0)
            col = lax.broadcasted_iota(jnp.int32, (blk, blk), 1)
            s = jnp.where(col <= row, s, -jnp.inf)
        m_new = jnp.maximum(m, jnp.max(s, axis=-1, keepdims=True))
        alpha = jnp.exp(m - m_new)
        p = jnp.exp(s - m_new)
        l = alpha * l + jnp.sum(p, axis=-1, keepdims=True)
        acc = alpha * acc + jnp.dot(p.astype(BF16), v, preferred_element_type=F32)
        return m_new, l, acc

    def q_block(i, _):
        r0 = pl.multiple_of(i * blk, blk)
        q = q_ref[pl.ds(r0, blk), :]
        init = (jnp.full((blk, 1), -jnp.inf, F32), jnp.zeros((blk, 1), F32), jnp.zeros((blk, dv), F32))
        carry = lax.fori_loop(0, i, lambda j, c: kv_step(j, c, q, False), init)
        _, l, acc = kv_step(i, carry, q, True)
        o_ref[pl.ds(r0, blk), :] = acc / l
        return 0

    lax.fori_loop(0, n_blocks, q_block, 0)


def _mla_prompt_attention(q_cat, k_cat, v, bsz, seq):
    n_heads = v.shape[1] // V_HEAD_DIM
    return pl.pallas_call(
        functools.partial(_flash_kernel, blk=FLASH_BLOCK),
        grid=(bsz, n_heads),
        in_specs=[pl.BlockSpec((seq, A_QK_PAD), lambda b, h: (b, h)),
                  pl.BlockSpec((seq, A_QK_PAD), lambda b, h: (b, h)),
                  pl.BlockSpec((seq, V_HEAD_DIM), lambda b, h: (b, h))],
        out_specs=pl.BlockSpec((seq, V_HEAD_DIM), lambda b, h: (b, h)),
        out_shape=jax.ShapeDtypeStruct((bsz * seq, n_heads * V_HEAD_DIM), F32),
        compiler_params=_cparams(("parallel", "parallel")),
        name="mla_prompt_flash",
    )(q_cat, k_cat, v)


def _paged_kernel(pt_ref, qlat_ref, qpe_ref, ckvn_ref, kpen_ref, ckv_hbm, kpe_hbm, o_ref,
                  ckv_buf, kpe_buf, sem, *, layer, ppc, n_split):
    n_samples, n_pages = pt_ref.shape
    n_chunks = n_pages // ppc
    pps = ppc // n_split

    def page_copies(page_of, slot):
        for p in range(ppc):
            page = page_of(p)
            yield pltpu.make_async_copy(ckv_hbm.at[layer, page], ckv_buf.at[slot, p], sem.at[0, slot])
            yield pltpu.make_async_copy(kpe_hbm.at[layer, page], kpe_buf.at[slot, p], sem.at[1, slot])

    def start_chunk(b, c, slot):
        for cp in page_copies(lambda p: pt_ref[b, c * ppc + p], slot):
            cp.start()

    def wait_chunk(slot):
        for cp in page_copies(lambda p: 0, slot):
            cp.wait()

    start_chunk(0, 0, 0)

    def sample_body(b, _):
        q = qlat_ref[b]
        qp = qpe_ref[b]
        ck_new = ckvn_ref[pl.ds(b, 1), :]
        kp_new = kpen_ref[pl.ds(b, 1), :]
        s_new = (jnp.sum(q * ck_new, axis=-1, keepdims=True)
                 + jnp.sum(qp * kp_new, axis=-1, keepdims=True)) * MLA_SCALE
        q_bf = q.astype(BF16)
        qp_bf = qp.astype(BF16)
        init = (s_new, jnp.ones_like(s_new), jnp.broadcast_to(ck_new, q.shape))

        def chunk_body(c, carry):
            m, l, acc = carry
            slot = lax.rem(c, 2)
            c1 = c + 1
            wrap = c1 == n_chunks
            nb = b + jnp.where(wrap, 1, 0)
            nc = jnp.where(wrap, 0, c1)

            @pl.when(nb < n_samples)
            def _():
                start_chunk(nb, nc, 1 - slot)

            wait_chunk(slot)
            dn = (((1,), (1,)), ((), ()))
            kvs, scores = [], []
            for j in range(n_split):
                kv = ckv_buf[slot, j * pps:(j + 1) * pps].reshape(pps * PAGE_SIZE, KV_LORA_RANK).astype(BF16)
                s_pe = jnp.concatenate(
                    [jnp.dot(qp_bf, kpe_buf[slot, j * pps + i].astype(BF16), preferred_element_type=F32)
                     for i in range(pps)], axis=1)
                s_lat = lax.dot_general(q_bf, kv, dn, preferred_element_type=F32)
                kvs.append(kv)
                scores.append((s_lat + s_pe) * MLA_SCALE)
            m_new = m
            for s in scores:
                m_new = jnp.maximum(m_new, jnp.max(s, axis=-1, keepdims=True))
            alpha = jnp.exp(m - m_new)
            l = alpha * l
            acc = alpha * acc
            for s, kv in zip(scores, kvs):
                p = jnp.exp(s - m_new)
                l = l + jnp.sum(p, axis=-1, keepdims=True)
                acc = acc + jnp.dot(p.astype(BF16), kv, preferred_element_type=F32)
            return m_new, l, acc

        _, l, acc = lax.fori_loop(0, n_chunks, chunk_body, init)
        o_ref[b] = acc / l
        return 0

    lax.fori_loop(0, n_samples, sample_body, 0)


def _mla_sample_attention(page_table, q_lat, q_pe, ckv_new, kpe_new, cache_ckv, cache_kpe_t, layer,
                          ppc=SAMPLE_PAGES_PER_CHUNK, n_split=SAMPLE_KEY_GROUPS):
    bsz, n_heads, rank = q_lat.shape
    assert page_table.shape[1] % (2 * ppc) == 0 and ppc % n_split == 0
    vmem = pl.BlockSpec(memory_space=pltpu.VMEM)
    return pl.pallas_call(
        functools.partial(_paged_kernel, layer=layer, ppc=ppc, n_split=n_split),
        in_specs=[pl.BlockSpec(memory_space=pltpu.SMEM), vmem, vmem, vmem, vmem,
                  pl.BlockSpec(memory_space=pl.ANY), pl.BlockSpec(memory_space=pl.ANY)],
        out_specs=vmem,
        out_shape=jax.ShapeDtypeStruct((bsz, n_heads, rank), F32),
        scratch_shapes=[pltpu.VMEM((2, ppc, PAGE_SIZE, rank), F32),
                        pltpu.VMEM((2, ppc, QK_ROPE_DIM, PAGE_SIZE), F32),
                        pltpu.SemaphoreType.DMA((2, 2))],
        compiler_params=pltpu.CompilerParams(vmem_limit_bytes=V7X_VMEM_LIMIT_BYTES),
        name="mla_sample_paged",
    )(page_table, q_lat, q_pe, ckv_new, kpe_new, cache_ckv, cache_kpe_t)


def _split3_bf16(x):
    x1 = x.astype(BF16)
    r1 = x - x1.astype(F32)
    x2 = r1.astype(BF16)
    x3 = (r1 - x2.astype(F32)).astype(BF16)
    return x1, x2, x3


def _dot_f32_by_01(x, mat01):
    x1, x2, x3 = _split3_bf16(x)
    out = jnp.dot(x3, mat01, preferred_element_type=F32)
    out += jnp.dot(x2, mat01, preferred_element_type=F32)
    out += jnp.dot(x1, mat01, preferred_element_type=F32)
    return out


def _dot_01_by_f32(mat01, x):
    x1, x2, x3 = _split3_bf16(x)
    out = jnp.dot(mat01, x3, preferred_element_type=F32)
    out += jnp.dot(mat01, x2, preferred_element_type=F32)
    out += jnp.dot(mat01, x1, preferred_element_type=F32)
    return out


def _ssd_kernel(xm_ref, bm_ref, cm_ref, dt_ref, dtt_ref, a_row_ref, a_col_ref, y_ref, st_ref, state):
    c = pl.program_id(1)
    q = xm_ref.shape[0]
    n_heads = dt_ref.shape[1]
    hp = xm_ref.shape[1] // n_heads
    n_state = state.shape[0]
    n_groups = bm_ref.shape[1] // n_state
    gw = xm_ref.shape[1] // n_groups
    e_heads = n_heads // n_groups

    @pl.when(c == 0)
    def _():
        state[...] = jnp.zeros_like(state)

    row = lax.broadcasted_iota(jnp.int32, (q, q), 0)
    col = lax.broadcasted_iota(jnp.int32, (q, q), 1)
    causal = col <= row
    tril = jnp.where(causal, 1.0, 0.0).astype(BF16)
    triu = jnp.where(row <= col, 1.0, 0.0).astype(BF16)
    erow = lax.broadcasted_iota(jnp.int32, (n_heads, n_heads * hp), 0)
    ecol = lax.broadcasted_iota(jnp.int32, (n_heads, n_heads * hp), 1)
    ehead = jnp.where(ecol >= erow * hp, jnp.where(ecol < (erow + 1) * hp, 1.0, 0.0), 0.0)
    expand = ehead.astype(BF16)

    dt = dt_ref[...]
    la = dt * a_row_ref[...]
    cum = _dot_01_by_f32(tril, la)
    cum_t = _dot_f32_by_01(dtt_ref[...] * a_col_ref[...], triu)
    cum_full = _dot_f32_by_01(cum, expand)
    dt_full = _dot_f32_by_01(dt, expand)
    last_full = cum_full[q - 1:q, :]
    decay_out = jnp.exp(cum_full)
    xm = xm_ref[...]
    xd = xm * dt_full
    xw = (xd * jnp.exp(last_full - cum_full)).astype(BF16)
    xd = xd.astype(BF16)
    chunk_decay = jnp.exp(last_full)

    dn_last = (((1,), (1,)), ((), ()))
    dn_first = (((0,), (0,)), ((), ()))
    for g in range(n_groups):
        bc = bm_ref[:, g * n_state:(g + 1) * n_state].astype(BF16)
        cc = cm_ref[:, g * n_state:(g + 1) * n_state].astype(BF16)
        cb = lax.dot_general(cc, bc, dn_last, preferred_element_type=F32)
        st_g = state[:, g * gw:(g + 1) * gw]
        y_off = jnp.dot(cc, st_g.astype(BF16), preferred_element_type=F32) * decay_out[:, g * gw:(g + 1) * gw]
        y_heads = []
        for e in range(e_heads):
            h = g * e_heads + e
            seg = jnp.exp(jnp.where(causal, cum[:, h:h + 1] - cum_t[h:h + 1, :], -jnp.inf))
            y_heads.append(jnp.dot((cb * seg).astype(BF16), xd[:, h * hp:(h + 1) * hp],
                                   preferred_element_type=F32))
        y_ref[:, g * gw:(g + 1) * gw] = jnp.concatenate(y_heads, axis=1) + y_off
        st_new = lax.dot_general(bc, xw[:, g * gw:(g + 1) * gw], dn_first, preferred_element_type=F32)
        state[:, g * gw:(g + 1) * gw] = st_g * chunk_decay[:, g * gw:(g + 1) * gw] + st_new

    @pl.when(c == pl.num_programs(1) - 1)
    def _():
        st_ref[...] = state[...]


def _ssd_prompt(xm, bm, cm, dt, a, bsz, seq):
    n_heads = dt.shape[1]
    hpw = xm.shape[1]
    gn = bm.shape[1]
    n_state = gn // M_GROUPS
    n_chunks = seq // CHUNK
    row_blk = lambda w: pl.BlockSpec((CHUNK, w), lambda b, c: (b * n_chunks + c, 0))
    y, st = pl.pallas_call(
        _ssd_kernel,
        grid=(bsz, n_chunks),
        in_specs=[row_blk(hpw), row_blk(gn), row_blk(gn), row_blk(n_heads),
                  pl.BlockSpec((n_heads, CHUNK), lambda b, c: (0, b * n_chunks + c)),
                  pl.BlockSpec((1, n_heads), lambda b, c: (0, 0)),
                  pl.BlockSpec((n_heads, 1), lambda b, c: (0, 0))],
        out_specs=[row_blk(hpw),
                   pl.BlockSpec((None, n_state, hpw), lambda b, c: (b, 0, 0))],
        out_shape=[jax.ShapeDtypeStruct((bsz * seq, hpw), F32),
                   jax.ShapeDtypeStruct((bsz, n_state, hpw), F32)],
        scratch_shapes=[pltpu.VMEM((n_state, hpw), F32)],
        compiler_params=_cparams(("parallel", "arbitrary")),
        name="ssd_prompt",
    )(xm, bm, cm, dt, dt.T, a.reshape(1, n_heads), a.reshape(n_heads, 1))
    return y, st


def _ret_log_gammas(n_heads):
    return [float(np.log1p(-np.exp2(np.float32(-5.0 - h), dtype=np.float32), dtype=np.float32))
            for h in range(n_heads)]


def _retention_kernel(q_ref, k_ref, v_ref, o_ref, s_ref, state):
    c = pl.program_id(1)
    qlen = q_ref.shape[0]
    n_heads = state.shape[0]
    dk = state.shape[1]
    dv = state.shape[2]

    @pl.when(c == 0)
    def _():
        state[...] = jnp.zeros_like(state)

    row = lax.broadcasted_iota(jnp.int32, (qlen, qlen), 0)
    col = lax.broadcasted_iota(jnp.int32, (qlen, qlen), 1)
    rel = (row - col).astype(F32)
    t_col = lax.broadcasted_iota(jnp.int32, (qlen, 1), 0).astype(F32)
    dn_last = (((1,), (1,)), ((), ()))
    dn_first = (((0,), (0,)), ((), ()))
    for h, lg in enumerate(_ret_log_gammas(n_heads)):
        q = q_ref[:, h * dk:(h + 1) * dk]
        k = k_ref[:, h * dk:(h + 1) * dk]
        v = v_ref[:, h * dv:(h + 1) * dv].astype(BF16)
        q_bf = q.astype(BF16)
        dmat = jnp.where(rel >= 0, jnp.exp(jnp.maximum(rel, 0.0) * lg), 0.0)
        scores = lax.dot_general(q_bf, k.astype(BF16), dn_last, preferred_element_type=F32) * dmat
        o_intra = jnp.dot(scores.astype(BF16), v, preferred_element_type=F32)
        k_w = (k * jnp.exp((qlen - 1 - t_col) * lg)).astype(BF16)
        st_new = lax.dot_general(k_w, v, dn_first, preferred_element_type=F32)
        s_prev = state[h]
        o_cross = jnp.dot(q_bf, s_prev.astype(BF16), preferred_element_type=F32) * jnp.exp((t_col + 1.0) * lg)
        o_ref[:, h * dv:(h + 1) * dv] = o_intra + o_cross
        state[h] = s_prev * float(np.exp(np.float32(qlen * lg))) + st_new

    @pl.when(c == pl.num_programs(1) - 1)
    def _():
        s_ref[...] = state[...]


def _retention_prompt(rq, rk, rv, bsz, seq):
    n_heads = rq.shape[1] // R_DK
    n_chunks = seq // CHUNK
    row_blk = lambda w: pl.BlockSpec((CHUNK, w), lambda b, c: (b * n_chunks + c, 0))
    return pl.pallas_call(
        _retention_kernel,
        grid=(bsz, n_chunks),
        in_specs=[row_blk(rq.shape[1]), row_blk(rk.shape[1]), row_blk(rv.shape[1])],
        out_specs=[row_blk(rv.shape[1]),
                   pl.BlockSpec((None, n_heads, R_DK, R_DV), lambda b, c: (b, 0, 0, 0))],
        out_shape=[jax.ShapeDtypeStruct((bsz * seq, rv.shape[1]), F32),
                   jax.ShapeDtypeStruct((bsz, n_heads, R_DK, R_DV), F32)],
        scratch_shapes=[pltpu.VMEM((n_heads, R_DK, R_DV), F32)],
        compiler_params=_cparams(("parallel", "arbitrary")),
        name="retention_prompt",
    )(rq, rk, rv)


def _rmsnorm(x, g):
    y = x * lax.rsqrt(jnp.mean(x * x, axis=-1, keepdims=True) + EPS)
    return y * g


def _gated_group_rmsnorm(y, z, g, n_groups):
    v = y * jax.nn.silu(z)
    shp = v.shape
    v = v.reshape(shp[:-1] + (n_groups, shp[-1] // n_groups))
    v = v * lax.rsqrt(jnp.mean(v * v, axis=-1, keepdims=True) + EPS)
    return v.reshape(shp) * g


def _head_groupnorm(o, g):
    mu = jnp.mean(o, axis=-1, keepdims=True)
    var = jnp.mean(jnp.square(o - mu), axis=-1, keepdims=True)
    o = (o - mu) * lax.rsqrt(var + GN_EPS)
    return o.reshape(o.shape[0], -1) * g


def _rope(x, pos):
    d = x.shape[-1]
    half = d // 2
    inv = ROPE_BASE ** (-2.0 * jnp.arange(half, dtype=F32) / d)
    ang = pos.astype(F32)[:, None] * inv[None, :]
    cos = jnp.cos(ang)[:, None, :]
    sin = jnp.sin(ang)[:, None, :]
    x1, x2 = x[..., :half], x[..., half:]
    return jnp.concatenate([x1 * cos - x2 * sin, x2 * cos + x1 * sin], axis=-1)


def _causal_dwconv_silu(full, w, b):
    L = full.shape[1] - (CONV_WIDTH - 1)
    out = b
    for t in range(CONV_WIDTH):
        out = out + full[:, t:t + L, :] * w[t]
    return jax.nn.silu(out)


def _segment(proj, idx):
    return proj[:, IN_OFFSETS_PAD[idx]:IN_OFFSETS_PAD[idx] + IN_WIDTHS[idx]]


def _branches(proj, q, pos, bsz, seq, conv_buf, ssm_h, ret_s, paged, lw):
    rows = bsz * seq
    z, xbc, dt_raw, _, ckv, kpe, rq, rk, rv, rg = (_segment(proj, i) for i in range(len(IN_WIDTHS)))
    full = jnp.concatenate([conv_buf, xbc.reshape(bsz, seq, CONV_DIM)], axis=1)
    new_conv = full[:, full.shape[1] - (CONV_WIDTH - 1):]
    xbc = _causal_dwconv_silu(full, lw['conv_w'], lw['conv_b']).reshape(rows, CONV_DIM)
    xm = xbc[:, :M_INNER]
    bm = xbc[:, M_INNER:M_INNER + M_GROUPS * M_STATE]
    cm = xbc[:, M_INNER + M_GROUPS * M_STATE:]
    dt = jax.nn.softplus(dt_raw + lw['dt_bias'])
    a = -jnp.exp(lw['a_log'])
    if paged is None:
        y, st = _ssd_prompt(xm, bm, cm, dt, a, bsz, seq)
        new_ssm = jnp.transpose(st.reshape(bsz, M_STATE, M_HEADS, M_HEAD_DIM), (0, 2, 3, 1))
    else:
        xdt = xm.reshape(rows, M_HEADS, M_HEAD_DIM) * dt[:, :, None]
        b_h = jnp.repeat(bm.reshape(rows, M_GROUPS, M_STATE), M_HEADS // M_GROUPS, axis=1)
        c_h = jnp.repeat(cm.reshape(rows, M_GROUPS, M_STATE), M_HEADS // M_GROUPS, axis=1)
        decay = jnp.exp(dt * a)
        y = (decay[:, :, None] * jnp.sum(ssm_h * c_h[:, :, None, :], axis=-1)
             + xdt * jnp.sum(b_h * c_h, axis=-1)[:, :, None]).reshape(rows, M_INNER)
        new_ssm = (decay, xdt, b_h)
    y = y + (lw['d_skip'][:, None] * xm.reshape(rows, M_HEADS, M_HEAD_DIM)).reshape(rows, M_INNER)
    y_m = _gated_group_rmsnorm(y, z, lw['m_norm_g'], M_GROUPS).astype(BF16)
    q = q.reshape(rows, A_HEADS, QK_NOPE_DIM + QK_ROPE_DIM)
    q_nope = q[..., :QK_NOPE_DIM]
    q_pe = _rope(q[..., QK_NOPE_DIM:], pos)
    ckv = _rmsnorm(ckv, lw['kv_norm_g'])
    kpe = _rope(kpe[:, None, :], pos)[:, 0, :]
    if paged is None:
        zpad = jnp.zeros((rows, A_HEADS, A_QK_PAD - QK_NOPE_DIM - QK_ROPE_DIM), F32)
        q_cat = jnp.concatenate([q_nope, q_pe, zpad], axis=-1).astype(BF16).reshape(rows, A_HEADS * A_QK_PAD)
        kv_in = jnp.concatenate([ckv, kpe], axis=-1).astype(BF16)
        k_cat = _mm(kv_in, lw['w_kcat'], BF16)
        v = _mm(kv_in, lw['w_vcat'], BF16)
        y_a = _mla_prompt_attention(q_cat, k_cat, v, bsz, seq).astype(BF16)
    else:
        cache_ckv, cache_kpe_t, page_table, layer = paged
        q_lat = jnp.einsum('bhd,rhd->bhr', q_nope, lw['w_uk'])
        o_lat = _mla_sample_attention(page_table, q_lat, q_pe, ckv, kpe, cache_ckv, cache_kpe_t, layer)
        y_a = jnp.einsum('bhr,rhd->bhd', o_lat, lw['w_uv']).reshape(rows, A_WIDTH).astype(BF16)
    rq = _rope(rq.reshape(rows, R_HEADS, R_DK), pos)
    rk = _rope(rk.reshape(rows, R_HEADS, R_DK), pos) * (R_DK ** -0.5)
    if paged is None:
        o, new_ret = _retention_prompt(rq.reshape(rows, R_QK_WIDTH), rk.reshape(rows, R_QK_WIDTH), rv, bsz, seq)
        o = o.reshape(rows, R_HEADS, R_DV)
    else:
        gamma = _ret_gamma()
        rv_h = rv.reshape(rows, R_HEADS, R_DV)
        o = (gamma[None, :, None] * jnp.sum(rq[:, :, :, None] * ret_s, axis=2)
             + jnp.sum(rq * rk, axis=-1)[:, :, None] * rv_h)
        new_ret = (rk, rv_h)
    y_r = (jax.nn.silu(rg) * _head_groupnorm(o, lw['r_norm_g'])).astype(BF16)
    return (y_m, y_a, y_r), new_conv, new_ssm, new_ret, ckv, kpe


def _mixer_block(x, n_prompt, prompt_args, sample_args, lw):
    h = _rmsnorm_rows(x, lw['norm_g'], BF16)
    proj = _mm(h, lw['w_in'])
    cq = _rmsnorm(_segment(proj, 3), lw['q_norm_g']).astype(BF16)
    q = _mm(cq, lw['w_uq'])
    ys_p, *aux_p = _branches(proj[:n_prompt], q[:n_prompt], *prompt_args, lw)
    ys_s, *aux_s = _branches(proj[n_prompt:], q[n_prompt:], *sample_args, lw)
    y_m, y_a, y_r = (jnp.concatenate([yp, ys], axis=0) for yp, ys in zip(ys_p, ys_s))
    gates = _mm_bias_sigmoid(h, lw['w_gate'], lw['b_gate'])
    mix = _mm_merge3(y_m, y_a, y_r, lw['w_br_m'], lw['w_br_a'], lw['w_br_r'], gates)
    x_out = _mm_residual(mix, lw['w_out'], x)
    return x_out, aux_p, aux_s


def _ret_gamma():
    return jnp.exp(jnp.log1p(-jnp.exp2(-5.0 - jnp.arange(R_HEADS, dtype=F32))))


def _sample_state_updates(state_ssm, state_ret, ssm_upd, ret_upd):
    decay, xdt, b_h = (jnp.stack(v) for v in zip(*ssm_upd))
    s_ssm = state_ssm * decay[..., None, None] + xdt[..., None] * b_h[..., None, :]
    rk, rv_h = (jnp.stack(v) for v in zip(*ret_upd))
    s_ret = state_ret * _ret_gamma()[None, None, :, None, None] + rk[..., None] * rv_h[..., None, :]
    return s_ssm, s_ret


def _ffn_block(x, lw):
    h = _rmsnorm_rows(x, lw['norm_ffn_g'], BF16)
    a = _mm_swiglu(h, lw['w_ffn_gate'], lw['w_ffn_up'])
    return _mm_kgrid_residual(a, lw['w_ffn_down'], x, 2)


def _pad_cols(w, n):
    return jnp.pad(w, ((0, 0), (0, n - w.shape[1])))


def _layer_weights(l, norm_mix_g, w_in, conv_w, conv_b, dt_bias, a_log, d_skip, m_norm_g,
                   q_norm_g, w_uq, kv_norm_g, w_ukv, r_norm_g, w_br_m, w_br_a, w_br_r, w_gate, b_gate, w_out,
                   norm_ffn_g, w_ffn_gate, w_ffn_up, w_ffn_down):
    w_in_l = w_in[l].astype(BF16)
    segs = []
    off = 0
    for w, wp in zip(IN_WIDTHS, IN_WIDTHS_PAD):
        segs.append(_pad_cols(w_in_l[:, off:off + w], wp))
        off += w
    w3 = w_ukv[l].reshape(KV_LORA_RANK, A_HEADS, QK_NOPE_DIM + V_HEAD_DIM)
    w_uk, w_uv = w3[..., :QK_NOPE_DIM], w3[..., QK_NOPE_DIM:]
    eye = jnp.eye(QK_ROPE_DIM, dtype=F32)
    top = jnp.concatenate([w_uk, jnp.zeros((KV_LORA_RANK, A_HEADS, A_QK_PAD - QK_NOPE_DIM), F32)], axis=-1)
    bot = jnp.concatenate([jnp.zeros((QK_ROPE_DIM, A_HEADS, QK_NOPE_DIM), F32),
                           jnp.broadcast_to(eye[:, None, :], (QK_ROPE_DIM, A_HEADS, QK_ROPE_DIM)),
                           jnp.zeros((QK_ROPE_DIM, A_HEADS, A_QK_PAD - QK_NOPE_DIM - QK_ROPE_DIM), F32)], axis=-1)
    w_kcat = jnp.concatenate([top, bot], axis=0).reshape(KV_LORA_RANK + QK_ROPE_DIM, A_HEADS * A_QK_PAD)
    w_vcat = jnp.concatenate([w_uv.reshape(KV_LORA_RANK, A_WIDTH),
                              jnp.zeros((QK_ROPE_DIM, A_WIDTH), F32)], axis=0)
    return dict(
        norm_g=norm_mix_g[l], w_in=jnp.concatenate(segs, axis=1),
        conv_w=conv_w[l], conv_b=conv_b[l], dt_bias=dt_bias[l], a_log=a_log[l],
        d_skip=d_skip[l], m_norm_g=m_norm_g[l], q_norm_g=q_norm_g[l],
        w_uq=(w_uq.astype(BF16), l), kv_norm_g=kv_norm_g[l],
        w_kcat=w_kcat.astype(BF16), w_vcat=w_vcat.astype(BF16), w_uk=w_uk, w_uv=w_uv,
        r_norm_g=r_norm_g[l],
        w_br_m=(w_br_m.astype(BF16), l), w_br_a=(w_br_a.astype(BF16), l), w_br_r=(w_br_r.astype(BF16), l),
        w_gate=(w_gate.astype(BF16), l), b_gate=b_gate[l], w_out=(w_out.astype(BF16), l),
        norm_ffn_g=norm_ffn_g[l],
        w_ffn_gate=(w_ffn_gate.astype(BF16), l), w_ffn_up=(w_ffn_up.astype(BF16), l),
        w_ffn_down=(w_ffn_down.astype(BF16), l),
    )


def kernel(x_prompt, x_sample, cache_ckv, cache_kpe, state_conv, state_ssm, state_ret, page_table,
           norm_mix_g, w_in, conv_w, conv_b, dt_bias, a_log, d_skip, m_norm_g,
           q_norm_g, w_uq, kv_norm_g, w_ukv, r_norm_g,
           w_br_m, w_br_a, w_br_r, w_gate, b_gate, w_out,
           norm_ffn_g, w_ffn_gate, w_ffn_up, w_ffn_down, norm_final_g):
    bp, lp, _ = x_prompt.shape
    bs, ls, _ = x_sample.shape
    pos_p = jnp.tile(jnp.arange(lp, dtype=jnp.int32), bp)
    pos_s = jnp.tile(PAST_LEN + jnp.arange(ls, dtype=jnp.int32), bs)
    n_prompt = bp * lp
    x = jnp.concatenate([x_prompt.reshape(n_prompt, D_MODEL), x_sample.reshape(bs * ls, D_MODEL)], axis=0)
    cache_kpe_t = jnp.swapaxes(cache_kpe, 2, 3)
    outs_p = [[] for _ in range(5)]
    outs_s = [[] for _ in range(5)]
    for l in range(DEPTH):
        lw = _layer_weights(l, norm_mix_g, w_in, conv_w, conv_b, dt_bias, a_log, d_skip, m_norm_g,
                            q_norm_g, w_uq, kv_norm_g, w_ukv, r_norm_g, w_br_m, w_br_a, w_br_r,
                            w_gate, b_gate, w_out, norm_ffn_g, w_ffn_gate, w_ffn_up, w_ffn_down)
        zc = jnp.zeros((bp, CONV_WIDTH - 1, CONV_DIM), F32)
        x, (c1, h1, r1, k1, e1), (c2, h2, r2, k2, e2) = _mixer_block(
            x, n_prompt, (pos_p, bp, lp, zc, None, None, None),
            (pos_s, bs, ls, state_conv[l], state_ssm[l], state_ret[l], (cache_ckv, cache_kpe_t, page_table, l)), lw)
        x = _ffn_block(x, lw)
        for lst, v in zip(outs_p, (k1.reshape(bp, lp, -1), e1.reshape(bp, lp, -1), c1, h1, r1)):
            lst.append(v)
        for lst, v in zip(outs_s, (k2.reshape(bs, ls, -1), e2.reshape(bs, ls, -1), c2, h2, r2)):
            lst.append(v)
    y = _rmsnorm_rows(x, norm_final_g, F32)
    y_prompt = y[:n_prompt].reshape(bp, lp, D_MODEL)
    y_sample = y[n_prompt:].reshape(bs, ls, D_MODEL)
    s_ssm, s_ret = _sample_state_updates(state_ssm, state_ret, outs_s[3], outs_s[4])
    return ((y_prompt, y_sample) + tuple(jnp.stack(v) for v in outs_p)
            + tuple(jnp.stack(v) for v in outs_s[:3]) + (s_ssm, s_ret))
```
